```python
import math
import jax, jax.numpy as jnp
from jax import lax
import numpy as np

D_MODEL = 4096
BATCH = 4
SEQ = 2048
DEPTH = 2
DEC_BATCH = 16
DEC_SEQ = 16
PAST_LEN = 2048

CHUNK = 64
Q_BLOCK = 128
FOX_HEADS = 16
FOX_HEAD_DIM = 128
FOX_WIDTH = FOX_HEADS * FOX_HEAD_DIM
MLA_HEADS = 16
NOPE_DIM = 128
ROPE_DIM = 64
V_DIM = 128
Q_LORA = 1024
KV_LORA = 512
MLA_WIDTH = MLA_HEADS * V_DIM
MIX_WIDTH = FOX_WIDTH + MLA_WIDTH
ROPE_THETA = 10000.0
FORGET_BIAS_INIT = 3.0
EPS = 1e-6
IN_SIZES = (FOX_WIDTH, FOX_WIDTH, FOX_WIDTH, FOX_HEADS, FOX_WIDTH,
            Q_LORA, KV_LORA, ROPE_DIM, MLA_WIDTH)
N_IN = sum(IN_SIZES)
IN_OFFSETS = tuple(int(o) for o in np.cumsum(IN_SIZES)[:-1])
FOX_SCALE = 1.0 / math.sqrt(FOX_HEAD_DIM)
MLA_SCALE = 1.0 / math.sqrt(NOPE_DIM + ROPE_DIM)

kernel_name = "fox_mla_parallel_heads_stream_step"


def rmsnorm(x, g):
    xf = x.astype(jnp.float32)
    xf = xf * lax.rsqrt(jnp.mean(xf * xf, axis=-1, keepdims=True) + EPS)
    return (xf * g.astype(jnp.float32)).astype(x.dtype)


def rope(x, pos):
    half = x.shape[-1] // 2
    inv_freq = 1.0 / (ROPE_THETA ** (jnp.arange(half, dtype=jnp.float32) / half))
    ang = pos.astype(jnp.float32)[:, None] * inv_freq[None, :]
    shape = (1, pos.shape[0]) + (1,) * (x.ndim - 3) + (half,)
    cos = jnp.cos(ang).reshape(shape)
    sin = jnp.sin(ang).reshape(shape)
    xf = x.astype(jnp.float32)
    x1, x2 = xf[..., :half], xf[..., half:]
    return jnp.concatenate([x1 * cos - x2 * sin, x2 * cos + x1 * sin], axis=-1).astype(x.dtype)


def swept_attention(score_fn, q_inputs, q_pos, k_pos, v, per_frame):
    sq = q_pos.shape[0]
    blk = min(Q_BLOCK, sq)
    nb = sq // blk

    def split(a):
        return jnp.moveaxis(a.reshape((a.shape[0], nb, blk) + a.shape[2:]), 1, 0)

    xs = tuple(split(a) for a in q_inputs) + (q_pos.reshape(nb, blk),)

    def one(args):
        qi, pi = args[:-1], args[-1]
        s = score_fn(*qi)
        if per_frame:
            allowed = k_pos[None, :] <= pi[:, None]
        else:
            allowed = (k_pos[None, :] // CHUNK) <= (pi[:, None] // CHUNK)
        s = jnp.where(allowed[None, None], s, -jnp.inf)
        p = jax.nn.softmax(s, axis=-1).astype(v.dtype)
        return jnp.einsum('bhqk,bkhd->bqhd', p, v)

    out = lax.map(one, xs)
    out = jnp.moveaxis(out, 0, 1)
    return out.reshape((out.shape[0], sq) + out.shape[3:])


def mixer_layer(x, p, past):
    (g_norm, w_in, b_f, g_q_fox, g_k_fox, g_cq, w_qb, g_qn, g_qp,
     g_ckv, g_kp, w_kvb, g_kn, w_out) = p
    B, S, _ = x.shape
    past_len = 0 if past is None else past[0].shape[1]
    q_pos = past_len + jnp.arange(S, dtype=jnp.int32)
    k_pos = jnp.arange(past_len + S, dtype=jnp.int32)

    h = rmsnorm(x, g_norm)
    u = h @ w_in
    q_a, k_a, v_a, f_a, z_a, cq, ckv, kpe, z_b = jnp.split(u, IN_OFFSETS, axis=-1)

    q_f = rmsnorm(q_a.reshape(B, S, FOX_HEADS, FOX_HEAD_DIM), g_q_fox)
    k_f = rmsnorm(k_a.reshape(B, S, FOX_HEADS, FOX_HEAD_DIM), g_k_fox)
    v_f = v_a.reshape(B, S, FOX_HEADS, FOX_HEAD_DIM)
    logf = jax.nn.log_sigmoid(f_a.astype(jnp.float32) + b_f.astype(jnp.float32))
    if past is None:
        k_all, v_all, logf_all = k_f, v_f, logf
    else:
        k_all = jnp.concatenate([past[0].astype(k_f.dtype), k_f], axis=1)
        v_all = jnp.concatenate([past[1].astype(v_f.dtype), v_f], axis=1)
        logf_all = jnp.concatenate([past[2].astype(jnp.float32), logf], axis=1)
    cum = jnp.cumsum(logf_all, axis=1)
    f_q = cum[:, past_len:]
    f_k_t = jnp.swapaxes(cum, 1, 2)

    def fox_score(qi, fqi):
        s = jnp.einsum('bqhd,bkhd->bhqk', qi, k_all).astype(jnp.float32) * FOX_SCALE
        return s + jnp.swapaxes(fqi, 1, 2)[..., None] - f_k_t[:, :, None, :]

    o_a = swept_attention(fox_score, (q_f, f_q), q_pos, k_pos, v_all, per_frame=True)
    o_a = o_a.reshape(B, S, FOX_WIDTH) * jax.nn.silu(z_a)

    cq = rmsnorm(cq, g_cq)
    qb = (cq @ w_qb).reshape(B, S, MLA_HEADS, NOPE_DIM + ROPE_DIM)
    q_nope = rmsnorm(qb[..., :NOPE_DIM], g_qn)
    q_pe = rope(rmsnorm(qb[..., NOPE_DIM:], g_qp), q_pos)
    ckv = rmsnorm(ckv, g_ckv)
    kpe = rope(rmsnorm(kpe, g_kp), q_pos)
    if past is None:
        ckv_all, kpe_all = ckv, kpe
    else:
        ckv_all = jnp.concatenate([past[3].astype(ckv.dtype), ckv], axis=1)
        kpe_all = jnp.concatenate([past[4].astype(kpe.dtype), kpe], axis=1)
    sk = ckv_all.shape[1]
    kv = (ckv_all @ w_kvb).reshape(B, sk, MLA_HEADS, NOPE_DIM + V_DIM)
    k_nope = rmsnorm(kv[..., :NOPE_DIM], g_kn)
    v_b = kv[..., NOPE_DIM:]

    def mla_score(qn, qp):
        s = (jnp.einsum('bqhd,bkhd->bhqk', qn, k_nope)
             + jnp.einsum('bqhr,bkr->bhqk', qp, kpe_all))
        return s.astype(jnp.float32) * MLA_SCALE

    o_b = swept_attention(mla_score, (q_nope, q_pe), q_pos, k_pos, v_b, per_frame=False)
    o_b = o_b.reshape(B, S, MLA_WIDTH) * jax.nn.silu(z_b)

    y = x + jnp.concatenate([o_a, o_b], axis=-1) @ w_out
    return y, (k_f, v_f, logf, ckv, kpe)


def setup_inputs(seed: int = 0) -> dict:
    key = jax.random.key(seed)
    ks = jax.random.split(key, 24)
    f32 = jnp.float32

    def nrm(k, shape, scale=1.0):
        return jax.random.normal(k, shape, f32) * scale

    def gain(k, shape):
        return 1.0 + 0.05 * jax.random.normal(k, shape, f32)

    L = DEPTH
    return {
        "x_prompt": nrm(ks[0], (BATCH, SEQ, D_MODEL)),
        "x_sample": nrm(ks[1], (DEC_BATCH, DEC_SEQ, D_MODEL)),
        "cache_fox_k": nrm(ks[2], (L, DEC_BATCH, PAST_LEN, FOX_HEADS, FOX_HEAD_DIM)),
        "cache_fox_v": nrm(ks[3], (L, DEC_BATCH, PAST_LEN, FOX_HEADS, FOX_HEAD_DIM)),
        "cache_fox_logf": jax.nn.log_sigmoid(FORGET_BIAS_INIT + nrm(ks[4], (L, DEC_BATCH, PAST_LEN, FOX_HEADS))),
        "cache_mla_ckv": nrm(ks[5], (L, DEC_BATCH, PAST_LEN, KV_LORA)),
        "cache_mla_kpe": nrm(ks[6], (L, DEC_BATCH, PAST_LEN, ROPE_DIM)),
        "g_norm": gain(ks[7], (L, D_MODEL)),
        "w_in": nrm(ks[8], (L, D_MODEL, N_IN), D_MODEL ** -0.5),
        "b_f": FORGET_BIAS_INIT + 0.1 * jax.random.normal(ks[9], (L, FOX_HEADS), f32),
        "g_q_fox": gain(ks[10], (L, FOX_HEAD_DIM)),
        "g_k_fox": gain(ks[11], (L, FOX_HEAD_DIM)),
        "g_cq": gain(ks[12], (L, Q_LORA)),
        "w_qb": nrm(ks[13], (L, Q_LORA, MLA_HEADS * (NOPE_DIM + ROPE_DIM)), Q_LORA ** -0.5),
        "g_qn": gain(ks[14], (L, NOPE_DIM)),
        "g_qp": gain(ks[15], (L, ROPE_DIM)),
        "g_ckv": gain(ks[16], (L, KV_LORA)),
        "g_kp": gain(ks[17], (L, ROPE_DIM)),
        "w_kvb": nrm(ks[18], (L, KV_LORA, MLA_HEADS * (NOPE_DIM + V_DIM)), KV_LORA ** -0.5),
        "g_kn": gain(ks[19], (L, NOPE_DIM)),
        "w_out": nrm(ks[20], (L, MIX_WIDTH, D_MODEL), MIX_WIDTH ** -0.5),
    }


def reference(x_prompt, x_sample, cache_fox_k, cache_fox_v, cache_fox_logf, cache_mla_ckv,
              cache_mla_kpe, g_norm, w_in, b_f, g_q_fox, g_k_fox, g_cq, w_qb, g_qn, g_qp,
              g_ckv, g_kp, w_kvb, g_kn, w_out):
    y_p, y_s = x_prompt, x_sample
    rows_p, rows_s = [], []
    for l in range(DEPTH):
        p = (g_norm[l], w_in[l], b_f[l], g_q_fox[l], g_k_fox[l], g_cq[l], w_qb[l], g_qn[l],
             g_qp[l], g_ckv[l], g_kp[l], w_kvb[l], g_kn[l], w_out[l])
        y_p, r_p = mixer_layer(y_p, p, None)
        past = (cache_fox_k[l], cache_fox_v[l], cache_fox_logf[l], cache_mla_ckv[l], cache_mla_kpe[l])
        y_s, r_s = mixer_layer(y_s, p, past)
        rows_p.append(r_p)
        rows_s.append(r_s)

    def stk(rows, i):
        return jnp.stack([r[i] for r in rows], axis=0)

    return (y_p, y_s,
            stk(rows_p, 0), stk(rows_p, 1), stk(rows_p, 2), stk(rows_p, 3), stk(rows_p, 4),
            stk(rows_s, 0), stk(rows_s, 1), stk(rows_s, 2), stk(rows_s, 3), stk(rows_s, 4))
```

```python
import functools
import math

import jax
import jax.numpy as jnp
from jax import lax
from jax.experimental import pallas as pl
from jax.experimental.pallas import tpu as pltpu

F32 = jnp.float32
BF16 = jnp.bfloat16

CHUNK = 64
HEADS = 16
HEAD_DIM = 128
WIDTH = HEADS * HEAD_DIM
NOPE_DIM = 128
ROPE_DIM = 64
Q_LORA = 1024
KV_LORA = 512
ROPE_THETA = 10000.0
EPS = 1e-6
FOX_SCALE = 1.0 / math.sqrt(HEAD_DIM)
MLA_SCALE = 1.0 / math.sqrt(NOPE_DIM + ROPE_DIM)
LANES = 128
NEW_PAD = 128
VMEM_LIMIT_BYTES = 52 * 1024 * 1024
NEG_INF = float("-inf")


def _cparams(*sem):
    return pltpu.CompilerParams(dimension_semantics=sem, vmem_limit_bytes=VMEM_LIMIT_BYTES)


def _row_tile(t, cap):
    if t <= cap:
        return t
    for c in range(cap, 15, -16):
        if t % c == 0:
            return c
    return t


def _rms(x, n=None):
    n = x.shape[-1] if n is None else n
    ms = jnp.sum(x * x, axis=-1, keepdims=True) * (1.0 / n)
    return x * lax.rsqrt(ms + EPS)


def _silu(x):
    return x * (1.0 / (1.0 + jnp.exp(-x)))


def _log_sigmoid(x):
    return jnp.minimum(x, 0.0) - jnp.log1p(jnp.exp(-jnp.abs(x)))


def _rmsnorm_call(x, g, name):
    t, d = x.shape
    tm = _row_tile(t, 256)

    def body(x_ref, g_ref, o_ref):
        o_ref[...] = (_rms(x_ref[...]) * g_ref[...]).astype(BF16)

    return pl.pallas_call(
        body,
        out_shape=jax.ShapeDtypeStruct((t, d), BF16),
        grid=(t // tm,),
        in_specs=[pl.BlockSpec((tm, d), lambda i: (i, 0)), pl.BlockSpec((1, d), lambda i: (0, 0))],
        out_specs=pl.BlockSpec((tm, d), lambda i: (i, 0)),
        compiler_params=_cparams("parallel"),
        name=name,
    )(x, g)


def _proj_call(x, w, tm, tn, extras, extra_specs, out_shapes, out_specs, epilogue, name,
               alias_inputs=(), scratch_shapes=()):
    t, k = x.shape
    n = w.shape[1]
    n_ex, n_al, n_out = len(extras), len(alias_inputs), len(out_shapes)

    def body(x_ref, w_ref, *refs):
        acc = jnp.dot(x_ref[...], w_ref[...], preferred_element_type=F32)
        ex = refs[:n_ex]
        outs = refs[n_ex + n_al:n_ex + n_al + n_out]
        scr = refs[n_ex + n_al + n_out:]
        epilogue(acc, x_ref, ex, outs, scr)

    in_specs = [pl.BlockSpec((tm, k), lambda i, j: (i, 0)), pl.BlockSpec((k, tn), lambda i, j: (0, j))]
    in_specs += list(extra_specs)
    in_specs += [pl.BlockSpec(memory_space=pl.ANY) for _ in alias_inputs]
    aliases = {2 + n_ex + a: out_idx for a, (_, out_idx) in enumerate(alias_inputs)}
    return pl.pallas_call(
        body,
        out_shape=out_shapes,
        grid=(t // tm, n // tn),
        in_specs=in_specs,
        out_specs=out_specs,
        scratch_shapes=list(scratch_shapes),
        input_output_aliases=aliases,
        compiler_params=_cparams("arbitrary", "arbitrary"),
        name=name,
    )(x, w, *extras, *[a for a, _ in alias_inputs])


def _head_out(nb, sl, tm, nh_t, head_major):
    if head_major:
        nsb = sl // tm
        shape = jax.ShapeDtypeStruct((nb, HEADS, sl, HEAD_DIM), BF16)
        spec = pl.BlockSpec((None, nh_t, tm, HEAD_DIM), lambda i, j: (i // nsb, j, i % nsb, 0))
    else:
        shape = jax.ShapeDtypeStruct((nb * sl, WIDTH), BF16)
        spec = pl.BlockSpec((tm, nh_t * HEAD_DIM), lambda i, j: (i, j))
    return shape, spec


def _store_head(o_ref, hh, val, head_major):
    if head_major:
        o_ref[hh] = val
    else:
        o_ref[:, hh * HEAD_DIM:(hh + 1) * HEAD_DIM] = val


def _const_spec(shape):
    nd = len(shape)
    return pl.BlockSpec(shape, lambda i, j: (0,) * nd)


def _proj_heads(h, w, g, scale, nb, sl, head_major, name, tm_cap=1024):
    t = h.shape[0]
    tm, tn = _row_tile(min(t, sl) if head_major else t, tm_cap), 512
    nh_t = tn // HEAD_DIM
    shape, spec = _head_out(nb, sl, tm, nh_t, head_major)

    def epi(acc, x_ref, ex, outs, scr):
        g_ref, = ex
        for hh in range(nh_t):
            v = _rms(acc[:, hh * HEAD_DIM:(hh + 1) * HEAD_DIM]) * g_ref[...] * scale
            _store_head(outs[0], hh, v.astype(BF16), head_major)

    return _proj_call(h, w, tm, tn, [g], [_const_spec((1, HEAD_DIM))], [shape], [spec], epi, name)[0]


def _proj_kv_fox(h, w, g, layer, prev, depth, nb, sl, head_major, name):
    t = h.shape[0]
    tm, tn = _row_tile(min(t, sl) if head_major else t, 512), 512
    nh_t = tn // HEAD_DIM
    f_shape = jax.ShapeDtypeStruct((depth, t * HEADS, HEAD_DIM), F32)
    f_spec = pl.BlockSpec((None, tm * HEADS, HEAD_DIM), lambda i, j: (layer, i, 0))
    shapes, specs = [f_shape], [f_spec]
    if head_major:
        b_shape, b_spec = _head_out(nb, sl, tm, nh_t, True)
        shapes.append(b_shape)
        specs.append(b_spec)
    extras = [] if g is None else [g]
    especs = [] if g is None else [_const_spec((1, HEAD_DIM))]

    def epi(acc, x_ref, ex, outs, scr):
        j = pl.program_id(1)
        for hh in range(nh_t):
            v = acc[:, hh * HEAD_DIM:(hh + 1) * HEAD_DIM]
            if g is not None:
                v = _rms(v) * ex[0][...]
            outs[0][pl.ds(j * nh_t + hh, tm, stride=HEADS), :] = v
            if head_major:
                outs[1][hh] = v.astype(BF16)

    alias = [] if prev is None else [(prev, 0)]
    return _proj_call(h, w, tm, tn, extras, especs, shapes, specs, epi, name, alias_inputs=alias)


def _proj_gate(h, w, name):
    t = h.shape[0]
    tm, tn = _row_tile(t, 1024), 512

    def epi(acc, x_ref, ex, outs, scr):
        outs[0][...] = _silu(acc).astype(BF16)

    return _proj_call(h, w, tm, tn, [], [], [jax.ShapeDtypeStruct((t, w.shape[1]), BF16)],
                      [pl.BlockSpec((tm, tn), lambda i, j: (i, j))], epi, name)[0]


def _proj_cq(h, w, g, name):
    t = h.shape[0]
    tm, tn = _row_tile(t, 1024), Q_LORA

    def epi(acc, x_ref, ex, outs, scr):
        outs[0][...] = (_rms(acc) * ex[0][...]).astype(BF16)

    return _proj_call(h, w, tm, tn, [g], [_const_spec((1, Q_LORA))], [jax.ShapeDtypeStruct((t, Q_LORA), BF16)],
                      [pl.BlockSpec((tm, tn), lambda i, j: (i, 0))], epi, name)[0]


def _proj_ckv(h, w, g, layer, prev, depth, name):
    t = h.shape[0]
    tm, tn = _row_tile(t, 1024), KV_LORA

    def epi(acc, x_ref, ex, outs, scr):
        v = _rms(acc) * ex[0][...]
        outs[0][...] = v
        outs[1][...] = v.astype(BF16)

    shapes = [jax.ShapeDtypeStruct((depth, t, KV_LORA), F32), jax.ShapeDtypeStruct((t, KV_LORA), BF16)]
    specs = [pl.BlockSpec((None, tm, tn), lambda i, j: (layer, i, 0)), pl.BlockSpec((tm, tn), lambda i, j: (i, 0))]
    alias = [] if prev is None else [(prev, 0)]
    return _proj_call(h, w, tm, tn, [g], [_const_spec((1, KV_LORA))], shapes, specs, epi, name, alias_inputs=alias)


def _swap_halves(x):
    lane = lax.broadcasted_iota(jnp.int32, x.shape, 1)
    return jnp.where((lane % ROPE_DIM) < ROPE_DIM // 2, pltpu.roll(x, LANES - ROPE_DIM // 2, 1),
                     pltpu.roll(x, ROPE_DIM // 2, 1))


def _proj_small(h, w_small, w_f_t, g_kp, b_f_row, b_f_col, cos, sin, layer, prev_kpe, prev_logf, depth,
                nb, sl, name):
    t = h.shape[0]
    tm = _row_tile(min(t, sl), 1024) if sl >= 128 else t
    nsb = max(sl // tm, 1)
    seqs_per_blk = max(tm // sl, 1)

    def epi(acc, x_ref, ex, outs, scr):
        g_ref, bfr_ref, bfc_ref, cos_ref, sin_ref, wft_ref = ex
        kpe_ref, kpe2_ref, logf_ref, cum_ref = outs
        carry_ref, = scr
        i = pl.program_id(0)
        kp = _rms(acc[:, :LANES], ROPE_DIM) * g_ref[...]
        kp = kp * cos_ref[...] + _swap_halves(kp) * sin_ref[...]
        kpe_ref[...] = kp[:, :ROPE_DIM]
        kpe2_ref[...] = (kp + pltpu.roll(kp, ROPE_DIM, 1)).astype(BF16)
        lf = _log_sigmoid(acc[:, LANES:2 * LANES] + bfr_ref[...])
        logf_ref[...] = lf[:, :HEADS]
        ut = lax.dot_general(wft_ref[...], x_ref[...], (((1,), (1,)), ((), ())), preferred_element_type=F32)
        c = _log_sigmoid(ut + bfc_ref[...])
        pos = lax.broadcasted_iota(jnp.int32, c.shape, 1) % sl
        sh = 1
        while sh < min(tm, sl):
            c = c + jnp.where(pos >= sh, pltpu.roll(c, sh, 1), 0.0)
            sh *= 2
        if nsb > 1:
            @pl.when(i % nsb == 0)
            def _():
                carry_ref[...] = jnp.zeros_like(carry_ref)
            c = c + carry_ref[:, :1]
            carry_ref[...] = jnp.broadcast_to(c[:, tm - 1:tm], carry_ref.shape)
            cum_ref[...] = c
        else:
            if cum_w != sl:
                cum_ref[...] = jnp.zeros(cum_ref.shape, F32)
            for s in range(seqs_per_blk):
                cum_ref[s, :, 0:sl] = c[:, s * sl:(s + 1) * sl]

    cum_w = max(sl, LANES)
    cum_shape = jax.ShapeDtypeStruct((nb, HEADS, cum_w), F32)
    if nsb > 1:
        cum_spec = pl.BlockSpec((None, HEADS, tm), lambda i, j: (i // nsb, 0, i % nsb))
    else:
        cum_spec = pl.BlockSpec((seqs_per_blk, HEADS, cum_w), lambda i, j: (i, 0, 0))
    shapes = [jax.ShapeDtypeStruct((depth, t, ROPE_DIM), F32), jax.ShapeDtypeStruct((t, LANES), BF16),
              jax.ShapeDtypeStruct((depth, t, HEADS), F32), cum_shape]
    specs = [pl.BlockSpec((None, tm, ROPE_DIM), lambda i, j: (layer, i, 0)),
             pl.BlockSpec((tm, LANES), lambda i, j: (i, 0)),
             pl.BlockSpec((None, tm, HEADS), lambda i, j: (layer, i, 0)), cum_spec]
    k = h.shape[1]
    extras = [g_kp, b_f_row, b_f_col, cos, sin, w_f_t]
    especs = [_const_spec((1, LANES)), _const_spec((1, LANES)), _const_spec((HEADS, 1)),
              pl.BlockSpec((tm, LANES), lambda i, j: (i, 0)), pl.BlockSpec((tm, LANES), lambda i, j: (i, 0)),
              _const_spec((HEADS, k))]
    alias = []
    if prev_kpe is not None:
        alias = [(prev_kpe, 0), (prev_logf, 2)]
    return _proj_call(h, w_small, tm, 2 * LANES, extras, especs, shapes, specs, epi, name, alias_inputs=alias,
                      scratch_shapes=[pltpu.VMEM((HEADS, LANES), F32)])


def _proj_q_pe(cq, w, g2, cos, sin, nb, sl, head_major, name):
    t = cq.shape[0]
    tm, tn = _row_tile(min(t, sl) if head_major else t, 512), HEADS * ROPE_DIM
    shape, spec = _head_out(nb, sl, tm, HEADS, head_major)

    def epi(acc, x_ref, ex, outs, scr):
        g_ref, cos_ref, sin_ref = ex
        lane = lax.broadcasted_iota(jnp.int32, (tm, LANES), 1)
        lo = lane < ROPE_DIM
        for p in range(HEADS // 2):
            x = acc[:, p * LANES:(p + 1) * LANES]
            x2 = x * x
            ms_lo = jnp.sum(jnp.where(lo, x2, 0.0), axis=-1, keepdims=True)
            ms_hi = jnp.sum(jnp.where(lo, 0.0, x2), axis=-1, keepdims=True)
            ms = jnp.where(lo, ms_lo, ms_hi) * (1.0 / ROPE_DIM)
            y = x * lax.rsqrt(ms + EPS) * g_ref[...]
            y = (y * cos_ref[...] + _swap_halves(y) * sin_ref[...]) * MLA_SCALE
            _store_head(outs[0], 2 * p, jnp.where(lo, y, 0.0).astype(BF16), head_major)
            _store_head(outs[0], 2 * p + 1, jnp.where(lo, 0.0, y).astype(BF16), head_major)

    especs = [_const_spec((1, LANES)), pl.BlockSpec((tm, LANES), lambda i, j: (i, 0)),
              pl.BlockSpec((tm, LANES), lambda i, j: (i, 0))]
    return _proj_call(cq, w, tm, tn, [g2, cos, sin], especs, [shape], [spec], epi, name)[0]


def _proj_kv_up(ckv, w, g, nb, sl, name):
    t = ckv.shape[0]
    tm, tn = _row_tile(min(t, sl), 1024), 512
    nh_t = tn // (2 * HEAD_DIM)
    nsb = sl // tm
    shape = jax.ShapeDtypeStruct((nb, HEADS, sl, HEAD_DIM), BF16)
    spec = pl.BlockSpec((None, nh_t, tm, HEAD_DIM), lambda i, j: (i // nsb, j, i % nsb, 0))

    def epi(acc, x_ref, ex, outs, scr):
        for hh in range(nh_t):
            base = hh * 2 * HEAD_DIM
            outs[0][hh] = (_rms(acc[:, base:base + HEAD_DIM]) * ex[0][...]).astype(BF16)
            outs[1][hh] = acc[:, base + HEAD_DIM:base + 2 * HEAD_DIM].astype(BF16)

    return _proj_call(ckv, w, tm, tn, [g], [_const_spec((1, HEAD_DIM))], [shape, shape], [spec, spec], epi, name)


def _flash_call(q_parts, k_parts, v, kbias, gate, gate_col0, nb, sl, chunked, name):
    tq = _row_tile(sl, 512)
    nqb = sl // tq
    t = nb * sl
    n_q, n_k = len(q_parts), len(k_parts)
    has_bias = kbias is not None

    def body(*refs):
        q_refs = refs[:n_q]
        k_refs = refs[n_q:n_q + n_k]
        v_ref = refs[n_q + n_k]
        pos = n_q + n_k + 1
        b_ref = refs[pos] if has_bias else None
        pos += int(has_bias)
        g_ref, o_ref = refs[pos], refs[pos + 1]
        h = pl.program_id(1)
        i = pl.program_id(2)
        q = jnp.concatenate([r[...] for r in q_refs], axis=-1) if n_q > 1 else q_refs[0][...]
        row = lax.broadcasted_iota(jnp.int32, (tq, tq), 0)
        col = lax.broadcasted_iota(jnp.int32, (tq, tq), 1)
        allowed = (col // CHUNK <= row // CHUNK) if chunked else (col <= row)

        def step(kb, carry, diagonal):
            m, l, acc = carry
            start = pl.multiple_of(kb * tq, tq)
            ks = [r[pl.ds(start, tq), :] for r in k_refs]
            k = jnp.concatenate(ks, axis=-1) if n_k > 1 else ks[0]
            s = lax.dot_general(q, k, (((1,), (1,)), ((), ())), preferred_element_type=F32)
            if has_bias:
                s = s - b_ref[pl.ds(h, 1), pl.ds(start, tq)]
            if diagonal:
                s = jnp.where(allowed, s, NEG_INF)
            m_new = jnp.maximum(m, jnp.max(s, axis=-1, keepdims=True))
            alpha = jnp.exp(m - m_new)
            p = jnp.exp(s - m_new)
            l = alpha * l + jnp.sum(p, axis=-1, keepdims=True)
            acc = alpha * acc + jnp.dot(p.astype(BF16), v_ref[pl.ds(start, tq), :], preferred_element_type=F32)
            return m_new, l, acc

        init = (jnp.full((tq, 1), NEG_INF, F32), jnp.zeros((tq, 1), F32), jnp.zeros((tq, HEAD_DIM), F32))
        carry = lax.fori_loop(0, i, lambda kb, c: step(kb, c, False), init)
        m, l, acc = step(i, carry, True)
        o_ref[...] = (acc / l * g_ref[...].astype(F32)).astype(BF16)

    def head_spec(a, rows):
        if a.ndim == 4:
            if rows == tq:
                return pl.BlockSpec((None, None, tq, a.shape[-1]), lambda b, h, i: (b, h, i, 0))
            return pl.BlockSpec((None, None, sl, a.shape[-1]), lambda b, h, i: (b, h, 0, 0))
        return pl.BlockSpec((None, sl, a.shape[-1]), lambda b, h, i: (b, 0, 0))

    in_specs = [head_spec(a, tq) for a in q_parts] + [head_spec(a, sl) for a in k_parts] + [head_spec(v, sl)]
    args = list(q_parts) + list(k_parts) + [v]
    if has_bias:
        in_specs.append(pl.BlockSpec((None, HEADS, sl), lambda b, h, i: (b, 0, 0)))
        args.append(kbias)
    in_specs.append(pl.BlockSpec((tq, HEAD_DIM), lambda b, h, i: (b * nqb + i, gate_col0 + h)))
    args.append(gate)
    return pl.pallas_call(
        body,
        out_shape=jax.ShapeDtypeStruct((t, WIDTH), BF16),
        grid=(nb, HEADS, nqb),
        in_specs=in_specs,
        out_specs=pl.BlockSpec((tq, HEAD_DIM), lambda b, h, i: (b * nqb + i, h)),
        compiler_params=_cparams("parallel", "parallel", "arbitrary"),
        name=name,
    )(*args)


def _pad_rows(x, rows):
    return jnp.concatenate([x, jnp.zeros((rows - x.shape[0], x.shape[1]), x.dtype)], axis=0)


def _fox_decode_call(q, cache_k, cache_v, new_k, new_v, layer, logf_t, cum_new, gate, nb, sd, past, name):
    tk = _row_tile(past, 256)
    nkc = past // tk
    ts = nb * sd

    def body(q_ref, ck_ref, cv_ref, nk_ref, nv_ref, lf_ref, cn_ref, g_ref, o_ref, cum_ref, m_ref, l_ref, acc_ref):
        kc = pl.program_id(1)

        @pl.when(kc == 0)
        def _():
            c = lf_ref[...]
            pos = lax.broadcasted_iota(jnp.int32, c.shape, 1)
            sh = 1
            while sh < past:
                c = c + jnp.where(pos >= sh, pltpu.roll(c, sh, 1), 0.0)
                sh *= 2
            cum_ref[...] = c
            m_ref[...] = jnp.full(m_ref.shape, NEG_INF, F32)
            l_ref[...] = jnp.zeros(l_ref.shape, F32)
            acc_ref[...] = jnp.zeros(acc_ref.shape, F32)

        def update(hh, s, v_bf):
            m_old = m_ref[hh]
            m_new = jnp.maximum(m_old, jnp.max(s, axis=-1, keepdims=True))
            alpha = jnp.exp(m_old - m_new)
            p = jnp.exp(s - m_new)
            l_ref[hh] = alpha * l_ref[hh] + jnp.sum(p, axis=-1, keepdims=True)
            acc_ref[hh] = alpha * acc_ref[hh] + jnp.dot(p.astype(BF16), v_bf, preferred_element_type=F32)
            m_ref[hh] = m_new

        start = pl.multiple_of(kc * tk, tk)
        for hh in range(HEADS):
            qh = q_ref[:, hh * HEAD_DIM:(hh + 1) * HEAD_DIM]
            kh = ck_ref[pl.ds(hh, tk, stride=HEADS), :].astype(BF16)
            vh = cv_ref[pl.ds(hh, tk, stride=HEADS), :].astype(BF16)
            s = lax.dot_general(qh, kh, (((1,), (1,)), ((), ())), preferred_element_type=F32)
            s = s - cum_ref[hh:hh + 1, pl.ds(start, tk)]
            update(hh, s, vh)

        @pl.when(kc == nkc - 1)
        def _():
            row = lax.broadcasted_iota(jnp.int32, (sd, NEW_PAD), 0)
            col = lax.broadcasted_iota(jnp.int32, (sd, NEW_PAD), 1)
            allowed = col <= row
            for hh in range(HEADS):
                qh = q_ref[:, hh * HEAD_DIM:(hh + 1) * HEAD_DIM]
                kh = _pad_rows(nk_ref[pl.ds(hh, sd, stride=HEADS), :].astype(BF16), NEW_PAD)
                vh = _pad_rows(nv_ref[pl.ds(hh, sd, stride=HEADS), :].astype(BF16), NEW_PAD)
                s = lax.dot_general(qh, kh, (((1,), (1,)), ((), ())), preferred_element_type=F32)
                total = cum_ref[hh:hh + 1, past - 1:past]
                bias = total + cn_ref[hh:hh + 1, :]
                s = jnp.where(allowed, s - bias, NEG_INF)
                update(hh, s, vh)
                out = acc_ref[hh] / l_ref[hh] * g_ref[:, hh * HEAD_DIM:(hh + 1) * HEAD_DIM].astype(F32)
                o_ref[:, hh * HEAD_DIM:(hh + 1) * HEAD_DIM] = out.astype(BF16)

    cache_spec = pl.BlockSpec((tk * HEADS, HEAD_DIM), lambda b, kc: ((layer * nb + b) * nkc + kc, 0))
    new_spec = pl.BlockSpec((None, sd * HEADS, HEAD_DIM), lambda b, kc: (layer, b, 0))
    return pl.pallas_call(
        body,
        out_shape=jax.ShapeDtypeStruct((ts, WIDTH), BF16),
        grid=(nb, nkc),
        in_specs=[
            pl.BlockSpec((sd, WIDTH), lambda b, kc: (b, 0)),
            cache_spec, cache_spec, new_spec, new_spec,
            pl.BlockSpec((None, None, HEADS, past), lambda b, kc: (layer, b, 0, 0)),
            pl.BlockSpec((None, HEADS, NEW_PAD), lambda b, kc: (b, 0, 0)),
            pl.BlockSpec((sd, WIDTH), lambda b, kc: (b, 0)),
        ],
        out_specs=pl.BlockSpec((sd, WIDTH), lambda b, kc: (b, 0)),
        scratch_shapes=[pltpu.VMEM((HEADS, past), F32), pltpu.VMEM((HEADS, sd, 1), F32),
                        pltpu.VMEM((HEADS, sd, 1), F32), pltpu.VMEM((HEADS, sd, HEAD_DIM), F32)],
        compiler_params=_cparams("parallel", "arbitrary"),
        name=name,
    )(q, cache_k, cache_v, new_k, new_v, logf_t, cum_new, gate)


def _mla_decode_call(q_nope, q_pe, cache_ckv, cache_kpe, new_ckv, new_kpe2, layer, w_kvb, g_kn, gate,
                     nb, sd, past, name):
    ts = nb * sd
    rows = past + NEW_PAD

    def body(qn_ref, qp_ref, cc_ref, ck_ref, nc_ref, nk_ref, w_ref, g_ref, gate_ref, o_ref, lat_ref, kpe_ref):
        hh = pl.program_id(1)

        @pl.when(hh == 0)
        def _():
            lat_ref[0:past, :] = cc_ref[...].astype(BF16)
            lat_ref[past:rows, :] = _pad_rows(nc_ref[...], NEW_PAD)
            kp = ck_ref[...].astype(BF16)
            kpe_ref[0:past, :] = jnp.concatenate([kp, kp], axis=-1)
            kpe_ref[past:rows, :] = _pad_rows(nk_ref[...], NEW_PAD)

        kv = jnp.dot(lat_ref[...], w_ref[...], preferred_element_type=F32)
        kn = (_rms(kv[:, :HEAD_DIM]) * g_ref[...]).astype(BF16)
        vv = kv[:, HEAD_DIM:].astype(BF16)
        k = jnp.concatenate([kn, kpe_ref[...]], axis=-1)
        q = jnp.concatenate([qn_ref[...], qp_ref[...]], axis=-1)
        s = lax.dot_general(q, k, (((1,), (1,)), ((), ())), preferred_element_type=F32)
        q_pos = past + lax.broadcasted_iota(jnp.int32, (sd, rows), 0)
        k_pos = lax.broadcasted_iota(jnp.int32, (sd, rows), 1)
        allowed = (k_pos // CHUNK <= q_pos // CHUNK) & (k_pos < past + sd)
        s = jnp.where(allowed, s, NEG_INF)
        m = jnp.max(s, axis=-1, keepdims=True)
        p = jnp.exp(s - m)
        l = jnp.sum(p, axis=-1, keepdims=True)
        out = jnp.dot(p.astype(BF16), vv, preferred_element_type=F32) / l
        o_ref[...] = (out * gate_ref[...].astype(F32)).astype(BF16)

    return pl.pallas_call(
        body,
        out_shape=jax.ShapeDtypeStruct((ts, WIDTH), BF16),
        grid=(nb, HEADS),
        in_specs=[
            pl.BlockSpec((sd, HEAD_DIM), lambda b, h: (b, h)),
            pl.BlockSpec((sd, HEAD_DIM), lambda b, h: (b, h)),
            pl.BlockSpec((None, past, KV_LORA), lambda b, h: (layer * nb + b, 0, 0)),
            pl.BlockSpec((None, past, ROPE_DIM), lambda b, h: (layer * nb + b, 0, 0)),
            pl.BlockSpec((sd, KV_LORA), lambda b, h: (b, 0)),
            pl.BlockSpec((sd, LANES), lambda b, h: (b, 0)),
            pl.BlockSpec((KV_LORA, 2 * HEAD_DIM), lambda b, h: (0, h)),
            pl.BlockSpec((1, HEAD_DIM), lambda b, h: (0, 0)),
            pl.BlockSpec((sd, HEAD_DIM), lambda b, h: (b, HEADS + h)),
        ],
        out_specs=pl.BlockSpec((sd, HEAD_DIM), lambda b, h: (b, h)),
        scratch_shapes=[pltpu.VMEM((rows, KV_LORA), BF16), pltpu.VMEM((rows, LANES), BF16)],
        compiler_params=_cparams("parallel", "arbitrary"),
        name=name,
    )(q_nope, q_pe, cache_ckv, cache_kpe, new_ckv, new_kpe2, w_kvb, g_kn, gate)


def _out_proj_call(o_a, o_b, w_out, x, name):
    t, d = x.shape
    tm, tn = _row_tile(t, 1024), 512

    def body(a_ref, b_ref, wa_ref, wb_ref, x_ref, y_ref):
        acc = jnp.dot(a_ref[...], wa_ref[...], preferred_element_type=F32)
        acc = acc + jnp.dot(b_ref[...], wb_ref[...], preferred_element_type=F32)
        y_ref[...] = x_ref[...] + acc

    return pl.pallas_call(
        body,
        out_shape=jax.ShapeDtypeStruct((t, d), F32),
        grid=(t // tm, d // tn),
        in_specs=[
            pl.BlockSpec((tm, WIDTH), lambda i, j: (i, 0)),
            pl.BlockSpec((tm, WIDTH), lambda i, j: (i, 0)),
            pl.BlockSpec((WIDTH, tn), lambda i, j: (0, j)),
            pl.BlockSpec((WIDTH, tn), lambda i, j: (1, j)),
            pl.BlockSpec((tm, tn), lambda i, j: (i, j)),
        ],
        out_specs=pl.BlockSpec((tm, tn), lambda i, j: (i, j)),
        compiler_params=_cparams("parallel", "parallel"),
        name=name,
    )(o_a, o_b, w_out, w_out, x)


def _rope_tables(pos):
    half = ROPE_DIM // 2
    inv_freq = 1.0 / (ROPE_THETA ** (jnp.arange(half, dtype=F32) / half))
    ang = pos.astype(F32)[:, None] * inv_freq[None, :]
    cos, sin = jnp.cos(ang), jnp.sin(ang)
    cos64 = jnp.concatenate([cos, cos], axis=-1)
    sin64 = jnp.concatenate([-sin, sin], axis=-1)
    return jnp.concatenate([cos64, cos64], axis=-1), jnp.concatenate([sin64, sin64], axis=-1)


def _layer_weights(l, g_norm, w_in, b_f, g_q_fox, g_k_fox, g_cq, w_qb, g_qn, g_qp, g_ckv, g_kp, w_kvb, g_kn,
                   w_out):
    w = w_in[l]
    o = 0
    seg = {}
    for name, size in (("q", WIDTH), ("k", WIDTH), ("v", WIDTH), ("f", HEADS), ("za", WIDTH),
                       ("cq", Q_LORA), ("ckv", KV_LORA), ("kpe", ROPE_DIM), ("zb", WIDTH)):
        seg[name] = w[:, o:o + size]
        o += size
    d = w.shape[0]
    w_small = jnp.concatenate([seg["kpe"], jnp.zeros((d, LANES - ROPE_DIM), F32), seg["f"],
                               jnp.zeros((d, LANES - HEADS), F32)], axis=1)
    qb = w_qb[l].reshape(Q_LORA, HEADS, NOPE_DIM + ROPE_DIM)
    row = lambda a: a.reshape(1, -1).astype(F32)
    pad = lambda a: jnp.concatenate([row(a), jnp.zeros((1, LANES - a.shape[-1]), F32)], axis=1)
    return dict(
        g_norm=row(g_norm[l]),
        wq=seg["q"].astype(BF16), wk=seg["k"].astype(BF16), wv=seg["v"].astype(BF16),
        wz=jnp.concatenate([seg["za"], seg["zb"]], axis=1).astype(BF16),
        wcq=seg["cq"].astype(BF16), wckv=seg["ckv"].astype(BF16),
        w_small=w_small.astype(BF16), w_f_t=seg["f"].T.astype(BF16),
        b_f_row=pad(b_f[l]), b_f_col=b_f[l].reshape(HEADS, 1).astype(F32),
        g_q=row(g_q_fox[l]), g_k=row(g_k_fox[l]), g_cq=row(g_cq[l]),
        w_qn=qb[:, :, :NOPE_DIM].reshape(Q_LORA, WIDTH).astype(BF16),
        w_qp=qb[:, :, NOPE_DIM:].reshape(Q_LORA, HEADS * ROPE_DIM).astype(BF16),
        g_qn=row(g_qn[l]), g_qp2=jnp.concatenate([row(g_qp[l]), row(g_qp[l])], axis=1),
        g_ckv=row(g_ckv[l]), g_kp=pad(g_kp[l]),
        w_kvb=w_kvb[l].astype(BF16), g_kn=row(g_kn[l]), w_out=w_out[l].astype(BF16),
    )


def _input_stage(x, p, cos, sin, layer, prev, depth, nb, sl, head_major, tag):
    h = _rmsnorm_call(x, p["g_norm"], f"norm_{tag}")
    prev = prev or {}
    r = {}
    r["q"] = _proj_heads(h, p["wq"], p["g_q"], FOX_SCALE, nb, sl, head_major, f"proj_q_{tag}")
    k_out = _proj_kv_fox(h, p["wk"], p["g_k"], layer, prev.get("k"), depth, nb, sl, head_major, f"proj_k_{tag}")
    v_out = _proj_kv_fox(h, p["wv"], None, layer, prev.get("v"), depth, nb, sl, head_major, f"proj_v_{tag}")
    r["k"], r["v"] = k_out[0], v_out[0]
    if head_major:
        r["k_bf"], r["v_bf"] = k_out[1], v_out[1]
    r["gate"] = _proj_gate(h, p["wz"], f"proj_gate_{tag}")
    cq = _proj_cq(h, p["wcq"], p["g_cq"], f"proj_cq_{tag}")
    r["ckv"], r["ckv_bf"] = _proj_ckv(h, p["wckv"], p["g_ckv"], layer, prev.get("ckv"), depth, f"proj_ckv_{tag}")
    r["kpe"], r["kpe2"], r["logf"], r["cum"] = _proj_small(
        h, p["w_small"], p["w_f_t"], p["g_kp"], p["b_f_row"], p["b_f_col"], cos, sin, layer,
        prev.get("kpe"), prev.get("logf"), depth, nb, sl, f"proj_small_{tag}")
    r["q_nope"] = _proj_heads(cq, p["w_qn"], p["g_qn"], MLA_SCALE, nb, sl, head_major, f"proj_qn_{tag}")
    r["q_pe"] = _proj_q_pe(cq, p["w_qp"], p["g_qp2"], cos, sin, nb, sl, head_major, f"proj_qp_{tag}")
    return r


def kernel(x_prompt, x_sample, cache_fox_k, cache_fox_v, cache_fox_logf, cache_mla_ckv, cache_mla_kpe, g_norm,
           w_in, b_f, g_q_fox, g_k_fox, g_cq, w_qb, g_qn, g_qp, g_ckv, g_kp, w_kvb, g_kn, w_out):
    nb, sl, d = x_prompt.shape
    nbd, sd, _ = x_sample.shape
    depth = w_in.shape[0]
    past = cache_fox_k.shape[2]
    assert sl % CHUNK == 0 and past % CHUNK == 0 and sd <= CHUNK, "chunk-aligned streaming shapes only"

    cos_p, sin_p = _rope_tables(jnp.arange(sl, dtype=jnp.int32))
    cos_p, sin_p = jnp.tile(cos_p, (nb, 1)), jnp.tile(sin_p, (nb, 1))
    cos_s, sin_s = _rope_tables(past + jnp.arange(sd, dtype=jnp.int32))
    cos_s, sin_s = jnp.tile(cos_s, (nbd, 1)), jnp.tile(sin_s, (nbd, 1))

    ck2 = cache_fox_k.reshape(-1, HEAD_DIM)
    cv2 = cache_fox_v.reshape(-1, HEAD_DIM)
    logf_t = jnp.swapaxes(cache_fox_logf, 2, 3)
    cckv = cache_mla_ckv.reshape(depth * nbd, past, KV_LORA)
    ckpe = cache_mla_kpe.reshape(depth * nbd, past, ROPE_DIM)

    y_p = x_prompt.reshape(nb * sl, d)
    y_s = x_sample.reshape(nbd * sd, d)
    prev_p, prev_s = None, None
    for l in range(depth):
        p = _layer_weights(l, g_norm, w_in, b_f, g_q_fox, g_k_fox, g_cq, w_qb, g_qn, g_qp, g_ckv, g_kp, w_kvb,
                           g_kn, w_out)
        rp = _input_stage(y_p, p, cos_p, sin_p, l, prev_p, depth, nb, sl, True, f"p{l}")
        kn_p, v_p = _proj_kv_up(rp["ckv_bf"], p["w_kvb"], p["g_kn"], nb, sl, f"kv_up_p{l}")
        o_a = _flash_call([rp["q"]], [rp["k_bf"]], rp["v_bf"], rp["cum"], rp["gate"], 0, nb, sl, False,
                          f"fox_attn_p{l}")
        o_b = _flash_call([rp["q_nope"], rp["q_pe"]], [kn_p, rp["kpe2"].reshape(nb, sl, LANES)], v_p, None,
                          rp["gate"], HEADS, nb, sl, True, f"mla_attn_p{l}")
        y_p = _out_proj_call(o_a, o_b, p["w_out"], y_p, f"out_proj_p{l}")
        prev_p = rp
        rs = _input_stage(y_s, p, cos_s, sin_s, l, prev_s, depth, nbd, sd, False, f"s{l}")
        o_a = _fox_decode_call(rs["q"], ck2, cv2, rs["k"], rs["v"], l, logf_t, rs["cum"], rs["gate"], nbd, sd,
                               past, f"fox_attn_s{l}")
        o_b = _mla_decode_call(rs["q_nope"], rs["q_pe"], cckv, ckpe, rs["ckv_bf"], rs["kpe2"], l, p["w_kvb"],
                               p["g_kn"], rs["gate"], nbd, sd, past, f"mla_attn_s{l}")
        y_s = _out_proj_call(o_a, o_b, p["w_out"], y_s, f"out_proj_s{l}")
        prev_s = rs

    hk = (HEADS, HEAD_DIM)
    return (y_p.reshape(nb, sl, d), y_s.reshape(nbd, sd, d),
            prev_p["k"].reshape((depth, nb, sl) + hk), prev_p["v"].reshape((depth, nb, sl) + hk),
            prev_p["logf"].reshape(depth, nb, sl, HEADS), prev_p["ckv"].reshape(depth, nb, sl, KV_LORA),
            prev_p["kpe"].reshape(depth, nb, sl, ROPE_DIM),
            prev_s["k"].reshape((depth, nbd, sd) + hk), prev_s["v"].reshape((depth, nbd, sd) + hk),
            prev_s["logf"].reshape(depth, nbd, sd, HEADS), prev_s["ckv"].reshape(depth, nbd, sd, KV_LORA),
            prev_s["kpe"].reshape(depth, nbd, sd, ROPE_DIM))
```

```python
import math

import jax
import jax.numpy as jnp
from jax import lax
from jax.experimental import pallas as pl
from jax.experimental.pallas import tpu as pltpu

F32 = jnp.float32
BF16 = jnp.bfloat16

CHUNK = 64
HEADS = 16
HEAD_DIM = 128
WIDTH = HEADS * HEAD_DIM
NOPE_DIM = 128
ROPE_DIM = 64
Q_LORA = 1024
KV_LORA = 512
ROPE_THETA = 10000.0
EPS = 1e-6
LOG2E = math.log2(math.e)
FOX_SCALE = LOG2E / math.sqrt(HEAD_DIM)
MLA_SCALE = LOG2E / math.sqrt(NOPE_DIM + ROPE_DIM)
LANES = 128
SUBLANES = 8
NEW_PAD = 128
VMEM_LIMIT_BYTES = 52 * 1024 * 1024
NEG_INF = float("-inf")

COL_Q, COL_K, COL_V, COL_GATE = 0, WIDTH, 2 * WIDTH, 3 * WIDTH
COL_CQ = 5 * WIDTH
COL_CKV = COL_CQ + Q_LORA
COL_SMALL = COL_CKV + KV_LORA
N_SMALL = 2 * LANES


def _cparams(*sem):
    return pltpu.CompilerParams(dimension_semantics=sem, vmem_limit_bytes=VMEM_LIMIT_BYTES)


def _row_tile(t, cap):
    if t <= cap:
        return t
    for c in range(cap, 15, -16):
        if t % c == 0:
            return c
    return t


def _rms(x, n=None, axis=-1):
    n = x.shape[axis] if n is None else n
    ms = jnp.sum(x * x, axis=axis, keepdims=True) * (1.0 / n)
    return x * lax.rsqrt(ms + EPS)


def _silu(x):
    return x * (1.0 / (1.0 + jnp.exp(-x)))


def _log_sigmoid(x):
    return jnp.minimum(x, 0.0) - jnp.log1p(jnp.exp(-jnp.abs(x)))


def _scan(x, axis, length, pos):
    sh = 1
    while sh < length:
        x = x + jnp.where(pos >= sh, pltpu.roll(x, sh, axis), 0.0)
        sh *= 2
    return x


def _rmsnorm_call(x, g, name):
    t, d = x.shape
    tm = _row_tile(t, 256)

    def body(x_ref, g_ref, o_ref):
        o_ref[...] = (_rms(x_ref[...]) * g_ref[...]).astype(BF16)

    return pl.pallas_call(
        body,
        out_shape=jax.ShapeDtypeStruct((t, d), BF16),
        grid=(t // tm,),
        in_specs=[pl.BlockSpec((tm, d), lambda i: (i, 0)), pl.BlockSpec((1, d), lambda i: (0, 0))],
        out_specs=pl.BlockSpec((tm, d), lambda i: (i, 0)),
        compiler_params=_cparams("parallel"),
        name=name,
    )(x, g)


def _proj_call(x, w, col0, n, tm, tn, extras, extra_specs, out_shapes, out_specs, epilogue, name,
               alias_inputs=(), scratch_shapes=()):
    t, k = x.shape
    assert col0 % tn == 0 and n % tn == 0 and t % tm == 0
    cb0 = col0 // tn
    n_ex, n_al, n_out = len(extras), len(alias_inputs), len(out_shapes)

    def body(x_ref, w_ref, *refs):
        acc = jnp.dot(x_ref[...], w_ref[...], preferred_element_type=F32)
        ex = refs[:n_ex]
        outs = refs[n_ex + n_al:n_ex + n_al + n_out]
        scr = refs[n_ex + n_al + n_out:]
        epilogue(acc, x_ref, ex, outs, scr)

    in_specs = [pl.BlockSpec((tm, k), lambda i, j: (i, 0)), pl.BlockSpec((k, tn), lambda i, j: (0, cb0 + j))]
    in_specs += list(extra_specs)
    in_specs += [pl.BlockSpec(memory_space=pl.ANY) for _ in alias_inputs]
    aliases = {2 + n_ex + a: out_idx for a, (_, out_idx) in enumerate(alias_inputs)}
    return pl.pallas_call(
        body,
        out_shape=out_shapes,
        grid=(t // tm, n // tn),
        in_specs=in_specs,
        out_specs=out_specs,
        scratch_shapes=list(scratch_shapes),
        input_output_aliases=aliases,
        compiler_params=_cparams("arbitrary", "arbitrary"),
        name=name,
    )(x, w, *extras, *[a for a, _ in alias_inputs])


def _const_spec(shape):
    nd = len(shape)
    return pl.BlockSpec(shape, lambda i, j: (0,) * nd)


def _head_out(nb, sl, tm, nh_t, head_major):
    if head_major:
        nsb = sl // tm
        shape = jax.ShapeDtypeStruct((nb, HEADS, sl, HEAD_DIM), BF16)
        spec = pl.BlockSpec((None, nh_t, tm, HEAD_DIM), lambda i, j: (i // nsb, j, i % nsb, 0))
    else:
        shape = jax.ShapeDtypeStruct((nb * sl, WIDTH), BF16)
        spec = pl.BlockSpec((tm, nh_t * HEAD_DIM), lambda i, j: (i, j))
    return shape, spec


def _store_head(o_ref, hh, val, head_major):
    if head_major:
        o_ref[hh] = val
    else:
        o_ref[:, hh * HEAD_DIM:(hh + 1) * HEAD_DIM] = val


def _proj_heads(h, w, col0, g, scale, nb, sl, head_major, name):
    t = h.shape[0]
    tm, tn = _row_tile(min(t, sl) if head_major else t, 1024), 512
    nh_t = tn // HEAD_DIM
    shape, spec = _head_out(nb, sl, tm, nh_t, head_major)

    def epi(acc, x_ref, ex, outs, scr):
        g_ref, = ex
        for hh in range(nh_t):
            v = _rms(acc[:, hh * HEAD_DIM:(hh + 1) * HEAD_DIM]) * g_ref[...] * scale
            _store_head(outs[0], hh, v.astype(BF16), head_major)

    return _proj_call(h, w, col0, WIDTH, tm, tn, [g], [_const_spec((1, HEAD_DIM))], [shape], [spec], epi, name)[0]


def _proj_kv_fox(h, w, col0, g, layer, prev, depth, nb, sl, head_major, name):
    t = h.shape[0]
    tm, tn = _row_tile(min(t, sl) if head_major else t, 512), 512
    nh_t = tn // HEAD_DIM
    f_shape = jax.ShapeDtypeStruct((depth, t * HEADS, HEAD_DIM), F32)
    f_spec = pl.BlockSpec((None, tm * HEADS, HEAD_DIM), lambda i, j: (layer, i, 0))
    shapes, specs = [f_shape], [f_spec]
    if head_major:
        b_shape, b_spec = _head_out(nb, sl, tm, nh_t, True)
        shapes.append(b_shape)
        specs.append(b_spec)
    extras = [] if g is None else [g]
    especs = [] if g is None else [_const_spec((1, HEAD_DIM))]

    def epi(acc, x_ref, ex, outs, scr):
        j = pl.program_id(1)
        for hh in range(nh_t):
            v = acc[:, hh * HEAD_DIM:(hh + 1) * HEAD_DIM]
            if g is not None:
                v = _rms(v) * ex[0][...]
            outs[0][pl.ds(j * nh_t + hh, tm, stride=HEADS), :] = v
            if head_major:
                outs[1][hh] = v.astype(BF16)

    alias = [] if prev is None else [(prev, 0)]
    return _proj_call(h, w, col0, WIDTH, tm, tn, extras, especs, shapes, specs, epi, name, alias_inputs=alias)


def _proj_gate(h, w, name):
    t = h.shape[0]
    tm, tn = _row_tile(t, 1024), 512

    def epi(acc, x_ref, ex, outs, scr):
        outs[0][...] = _silu(acc).astype(BF16)

    return _proj_call(h, w, COL_GATE, 2 * WIDTH, tm, tn, [], [], [jax.ShapeDtypeStruct((t, 2 * WIDTH), BF16)],
                      [pl.BlockSpec((tm, tn), lambda i, j: (i, j))], epi, name)[0]


def _proj_cq(h, w, g, name):
    t = h.shape[0]
    tm, tn = _row_tile(t, 1024), Q_LORA

    def epi(acc, x_ref, ex, outs, scr):
        outs[0][...] = (_rms(acc) * ex[0][...]).astype(BF16)

    return _proj_call(h, w, COL_CQ, Q_LORA, tm, tn, [g], [_const_spec((1, Q_LORA))],
                      [jax.ShapeDtypeStruct((t, Q_LORA), BF16)], [pl.BlockSpec((tm, tn), lambda i, j: (i, 0))],
                      epi, name)[0]


def _proj_ckv(h, w, g, layer, prev, depth, name):
    t = h.shape[0]
    tm, tn = _row_tile(t, 1024), KV_LORA

    def epi(acc, x_ref, ex, outs, scr):
        v = _rms(acc) * ex[0][...]
        outs[0][...] = v
        outs[1][...] = v.astype(BF16)

    shapes = [jax.ShapeDtypeStruct((depth, t, KV_LORA), F32), jax.ShapeDtypeStruct((t, KV_LORA), BF16)]
    specs = [pl.BlockSpec((None, tm, tn), lambda i, j: (layer, i, 0)), pl.BlockSpec((tm, tn), lambda i, j: (i, 0))]
    alias = [] if prev is None else [(prev, 0)]
    return _proj_call(h, w, COL_CKV, KV_LORA, tm, tn, [g], [_const_spec((1, KV_LORA))], shapes, specs, epi, name,
                      alias_inputs=alias)


def _swap_halves(x):
    lane = lax.broadcasted_iota(jnp.int32, x.shape, 1)
    return jnp.where((lane % ROPE_DIM) < ROPE_DIM // 2, pltpu.roll(x, LANES - ROPE_DIM // 2, 1),
                     pltpu.roll(x, ROPE_DIM // 2, 1))


def _proj_small(h, w, w_f_t, g_kp, b_f_row, b_f_col, cos, sin, layer, prev_kpe, prev_logf, depth, nb, sl,
                prompt, name):
    t, k = h.shape
    tm = _row_tile(min(t, sl), 1024) if prompt else t
    nsb = sl // tm if prompt else 1
    seqs_per_blk = 1 if prompt else tm // sl

    def epi(acc, x_ref, ex, outs, scr):
        g_ref, bfr_ref, bfc_ref, cos_ref, sin_ref, wft_ref = ex
        kpe_ref, kpe2_ref, logf_ref, cum_ref = outs
        i = pl.program_id(0)
        kp = _rms(acc[:, :LANES], ROPE_DIM) * g_ref[...]
        kp = kp * cos_ref[...] + _swap_halves(kp) * sin_ref[...]
        kpe_ref[...] = kp[:, :ROPE_DIM]
        kpe2_ref[...] = (kp + pltpu.roll(kp, ROPE_DIM, 1)).astype(BF16)
        lf = _log_sigmoid(acc[:, LANES:2 * LANES] + bfr_ref[...])
        logf_ref[...] = lf[:, :HEADS]
        if prompt:
            carry_ref, = scr
            lane = lax.broadcasted_iota(jnp.int32, lf.shape, 1)
            row = lax.broadcasted_iota(jnp.int32, lf.shape, 0)
            c = _scan(jnp.where(lane < HEADS, lf, 0.0), 0, tm, row)
            if nsb > 1:
                @pl.when(i % nsb == 0)
                def _():
                    carry_ref[...] = jnp.zeros_like(carry_ref)
                c = c + carry_ref[0:1, :]
                carry_ref[...] = jnp.broadcast_to(c[tm - 1:tm, :], carry_ref.shape)
            neg = c * (-LOG2E)
            hi = neg.astype(BF16).astype(F32)
            mid = (neg - hi).astype(BF16).astype(F32)
            lo = neg - hi - mid
            cum_ref[...] = (hi + pltpu.roll(mid, HEADS, 1) + pltpu.roll(lo, 2 * HEADS, 1)).astype(BF16)
        else:
            ut = lax.dot_general(wft_ref[...], x_ref[...], (((1,), (1,)), ((), ())), preferred_element_type=F32)
            c = _log_sigmoid(ut + bfc_ref[...])
            pos = lax.broadcasted_iota(jnp.int32, c.shape, 1) % sl
            c = _scan(c, 1, sl, pos) * (-LOG2E)
            cum_ref[...] = jnp.zeros(cum_ref.shape, F32)
            for s in range(seqs_per_blk):
                cum_ref[s, :, 0:sl] = c[:, s * sl:(s + 1) * sl]

    if prompt:
        cum_shape = jax.ShapeDtypeStruct((t, LANES), BF16)
        cum_spec = pl.BlockSpec((tm, LANES), lambda i, j: (i, 0))
        scratch = [pltpu.VMEM((SUBLANES, LANES), F32)]
    else:
        assert sl <= LANES
        cum_shape = jax.ShapeDtypeStruct((nb, HEADS, LANES), F32)
        cum_spec = pl.BlockSpec((seqs_per_blk, HEADS, LANES), lambda i, j: (i, 0, 0))
        scratch = []
    shapes = [jax.ShapeDtypeStruct((depth, t, ROPE_DIM), F32), jax.ShapeDtypeStruct((t, LANES), BF16),
              jax.ShapeDtypeStruct((depth, t, HEADS), F32), cum_shape]
    specs = [pl.BlockSpec((None, tm, ROPE_DIM), lambda i, j: (layer, i, 0)),
             pl.BlockSpec((tm, LANES), lambda i, j: (i, 0)),
             pl.BlockSpec((None, tm, HEADS), lambda i, j: (layer, i, 0)), cum_spec]
    extras = [g_kp, b_f_row, b_f_col, cos, sin, w_f_t]
    especs = [_const_spec((1, LANES)), _const_spec((1, LANES)), _const_spec((HEADS, 1)),
              pl.BlockSpec((tm, LANES), lambda i, j: (i, 0)), pl.BlockSpec((tm, LANES), lambda i, j: (i, 0)),
              _const_spec((HEADS, k))]
    alias = []
    if prev_kpe is not None:
        alias = [(prev_kpe, 0), (prev_logf, 2)]
    return _proj_call(h, w, COL_SMALL, N_SMALL, tm, N_SMALL, extras, especs, shapes, specs, epi, name,
                      alias_inputs=alias, scratch_shapes=scratch)


def _proj_q_pe(cq, w, g2, cos, sin, nb, sl, head_major, name):
    t = cq.shape[0]
    tm, tn = _row_tile(min(t, sl) if head_major else t, 512), HEADS * ROPE_DIM
    shape, spec = _head_out(nb, sl, tm, HEADS, head_major)

    def epi(acc, x_ref, ex, outs, scr):
        g_ref, cos_ref, sin_ref = ex
        lane = lax.broadcasted_iota(jnp.int32, (tm, LANES), 1)
        lo = lane < ROPE_DIM
        for p in range(HEADS // 2):
            x = acc[:, p * LANES:(p + 1) * LANES]
            x2 = x * x
            ms_lo = jnp.sum(jnp.where(lo, x2, 0.0), axis=-1, keepdims=True)
            ms_hi = jnp.sum(jnp.where(lo, 0.0, x2), axis=-1, keepdims=True)
            ms = jnp.where(lo, ms_lo, ms_hi) * (1.0 / ROPE_DIM)
            y = x * lax.rsqrt(ms + EPS) * g_ref[...]
            y = (y * cos_ref[...] + _swap_halves(y) * sin_ref[...]) * MLA_SCALE
            _store_head(outs[0], 2 * p, jnp.where(lo, y, 0.0).astype(BF16), head_major)
            _store_head(outs[0], 2 * p + 1, jnp.where(lo, 0.0, y).astype(BF16), head_major)

    especs = [_const_spec((1, LANES)), pl.BlockSpec((tm, LANES), lambda i, j: (i, 0)),
              pl.BlockSpec((tm, LANES), lambda i, j: (i, 0))]
    return _proj_call(cq, w, WIDTH, tn, tm, tn, [g2, cos, sin], especs, [shape], [spec], epi, name)[0]


def _proj_kv_up(ckv, w, g, nb, sl, name):
    t = ckv.shape[0]
    tm, tn = _row_tile(min(t, sl), 1024), 512
    nh_t = tn // (2 * HEAD_DIM)
    nsb = sl // tm
    shape = jax.ShapeDtypeStruct((nb, HEADS, sl, HEAD_DIM), BF16)
    spec = pl.BlockSpec((None, nh_t, tm, HEAD_DIM), lambda i, j: (i // nsb, j, i % nsb, 0))

    def epi(acc, x_ref, ex, outs, scr):
        for hh in range(nh_t):
            base = hh * 2 * HEAD_DIM
            outs[0][hh] = (_rms(acc[:, base:base + HEAD_DIM]) * ex[0][...]).astype(BF16)
            outs[1][hh] = acc[:, base + HEAD_DIM:base + 2 * HEAD_DIM].astype(BF16)

    return _proj_call(ckv, w, 0, 2 * WIDTH, tm, tn, [g], [_const_spec((1, HEAD_DIM))], [shape, shape],
                      [spec, spec], epi, name)


def _flash_call(q_parts, k_head, k_shared, v, gate, gate_col0, nb, sl, fox, name):
    tq = _row_tile(sl, 512)
    nqb = sl // tq
    t = nb * sl
    n_q = len(q_parts)

    def body(*refs):
        q_refs = refs[:n_q]
        kh_ref, ks_ref, v_ref, g_ref, o_ref = refs[n_q:]
        h = pl.program_id(1)
        key = lax.broadcasted_iota(jnp.int32, (tq, tq), 0)
        qry = lax.broadcasted_iota(jnp.int32, (tq, tq), 1)
        allowed = (key <= qry) if fox else (key // CHUNK <= qry // CHUNK)
        if fox:
            lane = lax.broadcasted_iota(jnp.int32, (tq, LANES), 1)
            selector = jnp.where((lane % HEADS == h) & (lane < 3 * HEADS), 1.0, 0.0).astype(BF16)

        def q_block(qi):
            rows = slice(qi * tq, (qi + 1) * tq)
            parts = [r[rows, :] for r in q_refs] + ([selector] if fox else [])
            return jnp.concatenate(parts, axis=-1)

        def k_block(kb):
            rows = slice(kb * tq, (kb + 1) * tq)
            return jnp.concatenate([kh_ref[rows, :], ks_ref[rows, :]], axis=-1)

        def scores(rnd):
            out = {}
            for qi in range(rnd, nqb):
                s = lax.dot_general(k_block(qi - rnd), q_block(qi), (((1,), (1,)), ((), ())),
                                    preferred_element_type=F32)
                out[qi] = jnp.where(allowed, s, NEG_INF) if rnd == 0 else s
            return out

        state = {qi: (jnp.full((1, tq), NEG_INF, F32), jnp.zeros((1, tq), F32), jnp.zeros((HEAD_DIM, tq), F32))
                 for qi in range(nqb)}
        s_cur = scores(0)
        for rnd in range(nqb):
            s_next = scores(rnd + 1) if rnd + 1 < nqb else {}
            for qi in range(rnd, nqb):
                m, l, acc = state[qi]
                s = s_cur[qi]
                m_new = jnp.maximum(m, jnp.max(s, axis=0, keepdims=True))
                alpha = jnp.exp2(m - m_new)
                p = jnp.exp2(s - m_new)
                l = alpha * l + jnp.sum(p, axis=0, keepdims=True)
                kb = qi - rnd
                pv = lax.dot_general(v_ref[kb * tq:(kb + 1) * tq, :], p.astype(BF16), (((0,), (0,)), ((), ())),
                                     preferred_element_type=F32)
                state[qi] = (m_new, l, alpha * acc + pv)
            s_cur = s_next
        for qi in range(nqb):
            m, l, acc = state[qi]
            rows = slice(qi * tq, (qi + 1) * tq)
            o_ref[rows, :] = ((acc / l).T * g_ref[rows, :].astype(F32)).astype(BF16)

    head_spec = pl.BlockSpec((None, None, sl, HEAD_DIM), lambda b, h: (b, h, 0, 0))
    in_specs = [head_spec for _ in q_parts] + [
        head_spec, pl.BlockSpec((None, sl, LANES), lambda b, h: (b, 0, 0)), head_spec,
        pl.BlockSpec((None, sl, HEAD_DIM), lambda b, h: (b, 0, gate_col0 + h))]
    return pl.pallas_call(
        body,
        out_shape=jax.ShapeDtypeStruct((nb, sl, WIDTH), BF16),
        grid=(nb, HEADS),
        in_specs=in_specs,
        out_specs=pl.BlockSpec((None, sl, HEAD_DIM), lambda b, h: (b, 0, h)),
        compiler_params=_cparams("parallel", "parallel"),
        name=name,
    )(*q_parts, k_head, k_shared, v, gate.reshape(nb, sl, 2 * WIDTH)).reshape(t, WIDTH)


def _pad_rows(x, rows):
    return jnp.concatenate([x, jnp.zeros((rows - x.shape[0], x.shape[1]), x.dtype)], axis=0)


def _fox_decode_call(q, cache_k, cache_v, new_k, new_v, layer, logf_t, cum_new, gate, nb, sd, past, name):
    tk = _row_tile(past, 512)
    nkc = past // tk
    ts = nb * sd
    rows = HEADS * sd

    def body(q_ref, ck_ref, cv_ref, nk_ref, nv_ref, lf_ref, cn_ref, g_ref, o_ref, cum_ref, m_ref, l_ref, acc_ref):
        kc = pl.program_id(1)

        @pl.when(kc == 0)
        def _():
            c = lf_ref[...]
            cum_ref[...] = _scan(c, 1, past, lax.broadcasted_iota(jnp.int32, c.shape, 1)) * (-LOG2E)
            m_ref[...] = jnp.full(m_ref.shape, NEG_INF, F32)
            l_ref[...] = jnp.zeros(l_ref.shape, F32)
            acc_ref[...] = jnp.zeros(acc_ref.shape, F32)

        def attend(keys, values, bias, mask):
            s = []
            for hh in range(HEADS):
                qh = q_ref[:, hh * HEAD_DIM:(hh + 1) * HEAD_DIM]
                sh = lax.dot_general(qh, keys(hh), (((1,), (1,)), ((), ())), preferred_element_type=F32)
                sh = sh + bias(hh)
                s.append(sh if mask is None else jnp.where(mask, sh, NEG_INF))
            s = jnp.concatenate(s, axis=0)
            m_old = m_ref[...]
            m_new = jnp.maximum(m_old, jnp.max(s, axis=-1, keepdims=True))
            alpha = jnp.exp2(m_old - m_new)
            p = jnp.exp2(s - m_new)
            l_ref[...] = alpha * l_ref[...] + jnp.sum(p, axis=-1, keepdims=True)
            m_ref[...] = m_new
            pb = p.astype(BF16)
            pv = [jnp.dot(pb[hh * sd:(hh + 1) * sd, :], values(hh), preferred_element_type=F32)
                  for hh in range(HEADS)]
            acc_ref[...] = alpha * acc_ref[...] + jnp.concatenate(pv, axis=0)

        start = pl.multiple_of(kc * tk, tk)
        attend(lambda hh: ck_ref[pl.ds(hh, tk, stride=HEADS), :].astype(BF16),
               lambda hh: cv_ref[pl.ds(hh, tk, stride=HEADS), :].astype(BF16),
               lambda hh: cum_ref[hh:hh + 1, pl.ds(start, tk)], None)

        @pl.when(kc == nkc - 1)
        def _():
            row = lax.broadcasted_iota(jnp.int32, (sd, NEW_PAD), 0)
            col = lax.broadcasted_iota(jnp.int32, (sd, NEW_PAD), 1)
            attend(lambda hh: _pad_rows(nk_ref[pl.ds(hh, sd, stride=HEADS), :].astype(BF16), NEW_PAD),
                   lambda hh: _pad_rows(nv_ref[pl.ds(hh, sd, stride=HEADS), :].astype(BF16), NEW_PAD),
                   lambda hh: cum_ref[hh:hh + 1, past - 1:past] + cn_ref[hh:hh + 1, :], col <= row)
            out = acc_ref[...] / l_ref[...]
            for hh in range(HEADS):
                gh = g_ref[:, hh * HEAD_DIM:(hh + 1) * HEAD_DIM].astype(F32)
                o_ref[:, hh * HEAD_DIM:(hh + 1) * HEAD_DIM] = (out[hh * sd:(hh + 1) * sd, :] * gh).astype(BF16)

    cache_spec = pl.BlockSpec((tk * HEADS, HEAD_DIM), lambda b, kc: ((layer * nb + b) * nkc + kc, 0))
    new_spec = pl.BlockSpec((None, sd * HEADS, HEAD_DIM), lambda b, kc: (layer, b, 0))
    return pl.pallas_call(
        body,
        out_shape=jax.ShapeDtypeStruct((ts, WIDTH), BF16),
        grid=(nb, nkc),
        in_specs=[
            pl.BlockSpec((sd, WIDTH), lambda b, kc: (b, 0)),
            cache_spec, cache_spec, new_spec, new_spec,
            pl.BlockSpec((None, None, HEADS, past), lambda b, kc: (layer, b, 0, 0)),
            pl.BlockSpec((None, HEADS, NEW_PAD), lambda b, kc: (b, 0, 0)),
            pl.BlockSpec((sd, WIDTH), lambda b, kc: (b, 0)),
        ],
        out_specs=pl.BlockSpec((sd, WIDTH), lambda b, kc: (b, 0)),
        scratch_shapes=[pltpu.VMEM((HEADS, past), F32), pltpu.VMEM((rows, 1), F32),
                        pltpu.VMEM((rows, 1), F32), pltpu.VMEM((rows, HEAD_DIM), F32)],
        compiler_params=_cparams("parallel", "arbitrary"),
        name=name,
    )(q, cache_k, cache_v, new_k, new_v, logf_t, cum_new, gate)


MLA_DECODE_HEADS_PER_STEP = 2


def _mla_decode_call(q_nope, q_pe, cache_ckv, cache_kpe, new_ckv, new_kpe2, layer, w_kvb, g_kn, gate,
                     nb, sd, past, name):
    ts = nb * sd
    rows = past + NEW_PAD
    hps = MLA_DECODE_HEADS_PER_STEP
    wblk = hps * HEAD_DIM

    def body(qn_ref, qp_ref, cc_ref, ck_ref, nc_ref, nk_ref, w_ref, g_ref, gate_ref, o_ref, lat_ref, kpe_ref):
        hp = pl.program_id(1)

        @pl.when(hp == 0)
        def _():
            lat_ref[0:past, :] = cc_ref[...].astype(BF16)
            lat_ref[past:rows, :] = _pad_rows(nc_ref[...], NEW_PAD)
            kp = ck_ref[...].astype(BF16)
            kpe_ref[0:past, :] = jnp.concatenate([kp, kp], axis=-1)
            kpe_ref[past:rows, :] = _pad_rows(nk_ref[...], NEW_PAD)

        kv = jnp.dot(lat_ref[...], w_ref[...], preferred_element_type=F32)
        q_pos = past + lax.broadcasted_iota(jnp.int32, (sd, rows), 0)
        k_pos = lax.broadcasted_iota(jnp.int32, (sd, rows), 1)
        allowed = (k_pos // CHUNK <= q_pos // CHUNK) & (k_pos < past + sd)
        for g in range(hps):
            base = g * 2 * HEAD_DIM
            kn = (_rms(kv[:, base:base + HEAD_DIM]) * g_ref[...]).astype(BF16)
            vv = kv[:, base + HEAD_DIM:base + 2 * HEAD_DIM].astype(BF16)
            k = jnp.concatenate([kn, kpe_ref[...]], axis=-1)
            hs = slice(g * HEAD_DIM, (g + 1) * HEAD_DIM)
            q = jnp.concatenate([qn_ref[:, hs], qp_ref[:, hs]], axis=-1)
            s = lax.dot_general(q, k, (((1,), (1,)), ((), ())), preferred_element_type=F32)
            s = jnp.where(allowed, s, NEG_INF)
            m = jnp.max(s, axis=-1, keepdims=True)
            p = jnp.exp2(s - m)
            l = jnp.sum(p, axis=-1, keepdims=True)
            out = jnp.dot(p.astype(BF16), vv, preferred_element_type=F32) / l
            o_ref[:, hs] = (out * gate_ref[:, hs].astype(F32)).astype(BF16)

    return pl.pallas_call(
        body,
        out_shape=jax.ShapeDtypeStruct((ts, WIDTH), BF16),
        grid=(nb, HEADS // hps),
        in_specs=[
            pl.BlockSpec((sd, wblk), lambda b, h: (b, h)),
            pl.BlockSpec((sd, wblk), lambda b, h: (b, h)),
            pl.BlockSpec((None, past, KV_LORA), lambda b, h: (layer * nb + b, 0, 0)),
            pl.BlockSpec((None, past, ROPE_DIM), lambda b, h: (layer * nb + b, 0, 0)),
            pl.BlockSpec((sd, KV_LORA), lambda b, h: (b, 0)),
            pl.BlockSpec((sd, LANES), lambda b, h: (b, 0)),
            pl.BlockSpec((KV_LORA, 2 * wblk), lambda b, h: (0, h)),
            pl.BlockSpec((1, HEAD_DIM), lambda b, h: (0, 0)),
            pl.BlockSpec((sd, wblk), lambda b, h: (b, HEADS // hps + h)),
        ],
        out_specs=pl.BlockSpec((sd, wblk), lambda b, h: (b, h)),
        scratch_shapes=[pltpu.VMEM((rows, KV_LORA), BF16), pltpu.VMEM((rows, LANES), BF16)],
        compiler_params=_cparams("parallel", "arbitrary"),
        name=name,
    )(q_nope, q_pe, cache_ckv, cache_kpe, new_ckv, new_kpe2, w_kvb, g_kn, gate)


def _out_proj_call(o_a, o_b, w_out, x, name):
    t, d = x.shape
    tm, tn = _row_tile(t, 1024), 512

    def body(a_ref, b_ref, wa_ref, wb_ref, x_ref, y_ref):
        acc = jnp.dot(a_ref[...], wa_ref[...], preferred_element_type=F32)
        acc = acc + jnp.dot(b_ref[...], wb_ref[...], preferred_element_type=F32)
        y_ref[...] = x_ref[...] + acc

    return pl.pallas_call(
        body,
        out_shape=jax.ShapeDtypeStruct((t, d), F32),
        grid=(t // tm, d // tn),
        in_specs=[
            pl.BlockSpec((tm, WIDTH), lambda i, j: (i, 0)),
            pl.BlockSpec((tm, WIDTH), lambda i, j: (i, 0)),
            pl.BlockSpec((WIDTH, tn), lambda i, j: (0, j)),
            pl.BlockSpec((WIDTH, tn), lambda i, j: (1, j)),
            pl.BlockSpec((tm, tn), lambda i, j: (i, j)),
        ],
        out_specs=pl.BlockSpec((tm, tn), lambda i, j: (i, j)),
        compiler_params=_cparams("parallel", "parallel"),
        name=name,
    )(o_a, o_b, w_out, w_out, x)


def _rope_tables(pos):
    half = ROPE_DIM // 2
    inv_freq = 1.0 / (ROPE_THETA ** (jnp.arange(half, dtype=F32) / half))
    ang = pos.astype(F32)[:, None] * inv_freq[None, :]
    cos, sin = jnp.cos(ang), jnp.sin(ang)
    cos64 = jnp.concatenate([cos, cos], axis=-1)
    sin64 = jnp.concatenate([-sin, sin], axis=-1)
    return jnp.concatenate([cos64, cos64], axis=-1), jnp.concatenate([sin64, sin64], axis=-1)


def _layer_weights(l, g_norm, w_in, b_f, g_q_fox, g_k_fox, g_cq, w_qb, g_qn, g_qp, g_ckv, g_kp, w_kvb, g_kn,
                   w_out):
    w = w_in[l]
    d = w.shape[0]
    o = 0
    seg = {}
    for name, size in (("q", WIDTH), ("k", WIDTH), ("v", WIDTH), ("f", HEADS), ("za", WIDTH),
                       ("cq", Q_LORA), ("ckv", KV_LORA), ("kpe", ROPE_DIM), ("zb", WIDTH)):
        seg[name] = w[:, o:o + size]
        o += size
    zeros = lambda n: jnp.zeros((d, n), F32)
    w_all = jnp.concatenate(
        [seg["q"], seg["k"], seg["v"], seg["za"], seg["zb"], seg["cq"], seg["ckv"],
         seg["kpe"], zeros(LANES - ROPE_DIM), seg["f"], zeros(LANES - HEADS)], axis=1).astype(BF16)
    qb = w_qb[l].reshape(Q_LORA, HEADS, NOPE_DIM + ROPE_DIM)
    w_q = jnp.concatenate([qb[:, :, :NOPE_DIM].reshape(Q_LORA, WIDTH),
                           qb[:, :, NOPE_DIM:].reshape(Q_LORA, HEADS * ROPE_DIM)], axis=1).astype(BF16)
    row = lambda a: a.reshape(1, -1).astype(F32)
    pad = lambda a: jnp.concatenate([row(a), jnp.zeros((1, LANES - a.shape[-1]), F32)], axis=1)
    return dict(
        g_norm=row(g_norm[l]), w_all=w_all, w_f_t=seg["f"].T.astype(BF16),
        b_f_row=pad(b_f[l]), b_f_col=b_f[l].reshape(HEADS, 1).astype(F32),
        g_q=row(g_q_fox[l]), g_k=row(g_k_fox[l]), g_cq=row(g_cq[l]), w_q=w_q,
        g_qn=row(g_qn[l]), g_qp2=jnp.concatenate([row(g_qp[l]), row(g_qp[l])], axis=1),
        g_ckv=row(g_ckv[l]), g_kp=pad(g_kp[l]),
        w_kvb=w_kvb[l].astype(BF16), g_kn=row(g_kn[l]), w_out=w_out[l].astype(BF16),
    )


def _input_stage(x, p, cos, sin, layer, prev, depth, nb, sl, prompt, tag):
    h = _rmsnorm_call(x, p["g_norm"], f"norm_{tag}")
    prev = prev or {}
    w = p["w_all"]
    r = {}
    r["q"] = _proj_heads(h, w, COL_Q, p["g_q"], FOX_SCALE, nb, sl, prompt, f"proj_q_{tag}")
    k_out = _proj_kv_fox(h, w, COL_K, p["g_k"], layer, prev.get("k"), depth, nb, sl, prompt, f"proj_k_{tag}")
    v_out = _proj_kv_fox(h, w, COL_V, None, layer, prev.get("v"), depth, nb, sl, prompt, f"proj_v_{tag}")
    r["k"], r["v"] = k_out[0], v_out[0]
    if prompt:
        r["k_bf"], r["v_bf"] = k_out[1], v_out[1]
    r["gate"] = _proj_gate(h, w, f"proj_gate_{tag}")
    cq = _proj_cq(h, w, p["g_cq"], f"proj_cq_{tag}")
    r["ckv"], r["ckv_bf"] = _proj_ckv(h, w, p["g_ckv"], layer, prev.get("ckv"), depth, f"proj_ckv_{tag}")
    r["kpe"], r["kpe2"], r["logf"], r["cum"] = _proj_small(
        h, w, p["w_f_t"], p["g_kp"], p["b_f_row"], p["b_f_col"], cos, sin, layer,
        prev.get("kpe"), prev.get("logf"), depth, nb, sl, prompt, f"proj_small_{tag}")
    r["q_nope"] = _proj_heads(cq, p["w_q"], 0, p["g_qn"], MLA_SCALE, nb, sl, prompt, f"proj_qn_{tag}")
    r["q_pe"] = _proj_q_pe(cq, p["w_q"], p["g_qp2"], cos, sin, nb, sl, prompt, f"proj_qp_{tag}")
    return r


def kernel(x_prompt, x_sample, cache_fox_k, cache_fox_v, cache_fox_logf, cache_mla_ckv, cache_mla_kpe, g_norm,
           w_in, b_f, g_q_fox, g_k_fox, g_cq, w_qb, g_qn, g_qp, g_ckv, g_kp, w_kvb, g_kn, w_out):
    nb, sl, d = x_prompt.shape
    nbd, sd, _ = x_sample.shape
    depth = w_in.shape[0]
    past = cache_fox_k.shape[2]
    assert sl % CHUNK == 0 and past % CHUNK == 0 and sd <= CHUNK, "chunk-aligned streaming shapes only"

    cos_p, sin_p = _rope_tables(jnp.arange(sl, dtype=jnp.int32))
    cos_p, sin_p = jnp.tile(cos_p, (nb, 1)), jnp.tile(sin_p, (nb, 1))
    cos_s, sin_s = _rope_tables(past + jnp.arange(sd, dtype=jnp.int32))
    cos_s, sin_s = jnp.tile(cos_s, (nbd, 1)), jnp.tile(sin_s, (nbd, 1))

    ck2 = cache_fox_k.reshape(-1, HEAD_DIM)
    cv2 = cache_fox_v.reshape(-1, HEAD_DIM)
    logf_t = jnp.swapaxes(cache_fox_logf, 2, 3)
    cckv = cache_mla_ckv.reshape(depth * nbd, past, KV_LORA)
    ckpe = cache_mla_kpe.reshape(depth * nbd, past, ROPE_DIM)

    y_p = x_prompt.reshape(nb * sl, d)
    y_s = x_sample.reshape(nbd * sd, d)
    prev_p, prev_s = None, None
    for l in range(depth):
        p = _layer_weights(l, g_norm, w_in, b_f, g_q_fox, g_k_fox, g_cq, w_qb, g_qn, g_qp, g_ckv, g_kp, w_kvb,
                           g_kn, w_out)
        rp = _input_stage(y_p, p, cos_p, sin_p, l, prev_p, depth, nb, sl, True, f"p{l}")
        kn_p, v_p = _proj_kv_up(rp["ckv_bf"], p["w_kvb"], p["g_kn"], nb, sl, f"kv_up_p{l}")
        o_a = _flash_call([rp["q"]], rp["k_bf"], rp["cum"].reshape(nb, sl, LANES), rp["v_bf"], rp["gate"], 0,
                          nb, sl, True, f"fox_attn_p{l}")
        o_b = _flash_call([rp["q_nope"], rp["q_pe"]], kn_p, rp["kpe2"].reshape(nb, sl, LANES), v_p, rp["gate"],
                          HEADS, nb, sl, False, f"mla_attn_p{l}")
        y_p = _out_proj_call(o_a, o_b, p["w_out"], y_p, f"out_proj_p{l}")
        prev_p = rp
        rs = _input_stage(y_s, p, cos_s, sin_s, l, prev_s, depth, nbd, sd, False, f"s{l}")
        o_a = _fox_decode_call(rs["q"], ck2, cv2, rs["k"], rs["v"], l, logf_t, rs["cum"], rs["gate"], nbd, sd,
                               past, f"fox_attn_s{l}")
        o_b = _mla_decode_call(rs["q_nope"], rs["q_pe"], cckv, ckpe, rs["ckv_bf"], rs["kpe2"], l, p["w_kvb"],
                               p["g_kn"], rs["gate"], nbd, sd, past, f"mla_attn_s{l}")
        y_s = _out_proj_call(o_a, o_b, p["w_out"], y_s, f"out_proj_s{l}")
        prev_s = rs

    hk = (HEADS, HEAD_DIM)
    return (y_p.reshape(nb, sl, d), y_s.reshape(nbd, sd, d),
            prev_p["k"].reshape((depth, nb, sl) + hk), prev_p["v"].reshape((depth, nb, sl) + hk),
            prev_p["logf"].reshape(depth, nb, sl, HEADS), prev_p["ckv"].reshape(depth, nb, sl, KV_LORA),
            prev_p["kpe"].reshape(depth, nb, sl, ROPE_DIM),
            prev_s["k"].reshape((depth, nbd, sd) + hk), prev_s["v"].reshape((depth, nbd, sd) + hk),
            prev_s["logf"].reshape(depth, nbd, sd, HEADS), prev_s["ckv"].reshape(depth, nbd, sd, KV_LORA),
            prev_s["kpe"].reshape(depth, nbd, sd, ROPE_DIM))
```

```python
import math

import jax
import jax.numpy as jnp
from jax import lax
from jax.experimental import pallas as pl
from jax.experimental.pallas import tpu as pltpu

F32 = jnp.float32
BF16 = jnp.bfloat16

CHUNK = 64
HEADS = 16
HEAD_DIM = 128
WIDTH = HEADS * HEAD_DIM
NOPE_DIM = 128
ROPE_DIM = 64
Q_LORA = 1024
KV_LORA = 512
ROPE_THETA = 10000.0
EPS = 1e-6
LOG2E = math.log2(math.e)
FOX_SCALE = LOG2E / math.sqrt(HEAD_DIM)
MLA_SCALE = LOG2E / math.sqrt(NOPE_DIM + ROPE_DIM)
LANES = 128
SUBLANES = 8
NEW_PAD = 128
VMEM_LIMIT_BYTES = 52 * 1024 * 1024
NEG_INF = float("-inf")

COL_Q, COL_K, COL_V, COL_GATE = 0, WIDTH, 2 * WIDTH, 3 * WIDTH
COL_CQ = 5 * WIDTH
COL_CKV = COL_CQ + Q_LORA
COL_SMALL = COL_CKV + KV_LORA
N_SMALL = 2 * LANES


def _cparams(*sem):
    return pltpu.CompilerParams(dimension_semantics=sem, vmem_limit_bytes=VMEM_LIMIT_BYTES)


def _row_tile(t, cap):
    if t <= cap:
        return t
    for c in range(cap, 15, -16):
        if t % c == 0:
            return c
    return t


def _rms(x, n=None, axis=-1):
    n = x.shape[axis] if n is None else n
    ms = jnp.sum(x * x, axis=axis, keepdims=True) * (1.0 / n)
    return x * lax.rsqrt(ms + EPS)


def _silu(x):
    return x * (1.0 / (1.0 + jnp.exp(-x)))


def _log_sigmoid(x):
    return jnp.minimum(x, 0.0) - jnp.log1p(jnp.exp(-jnp.abs(x)))


def _scan(x, axis, length, pos):
    sh = 1
    while sh < length:
        x = x + jnp.where(pos >= sh, pltpu.roll(x, sh, axis), 0.0)
        sh *= 2
    return x


def _rmsnorm_call(x, g, name):
    t, d = x.shape
    tm = _row_tile(t, 256)

    def body(x_ref, g_ref, o_ref):
        o_ref[...] = (_rms(x_ref[...]) * g_ref[...]).astype(BF16)

    return pl.pallas_call(
        body,
        out_shape=jax.ShapeDtypeStruct((t, d), BF16),
        grid=(t // tm,),
        in_specs=[pl.BlockSpec((tm, d), lambda i: (i, 0)), pl.BlockSpec((1, d), lambda i: (0, 0))],
        out_specs=pl.BlockSpec((tm, d), lambda i: (i, 0)),
        compiler_params=_cparams("parallel"),
        name=name,
    )(x, g)


def _proj_call(x, w, col0, n, tm, tn, extras, extra_specs, out_shapes, out_specs, epilogue, name,
               alias_inputs=(), scratch_shapes=(), layer=None, w_rows_are_outputs=False):
    t, k = x.shape
    assert col0 % tn == 0 and n % tn == 0 and t % tm == 0
    cb0 = col0 // tn
    n_ex, n_al, n_out = len(extras), len(alias_inputs), len(out_shapes)
    contract = (((1,), (1,)), ((), ())) if w_rows_are_outputs else (((1,), (0,)), ((), ()))

    def body(x_ref, w_ref, *refs):
        acc = lax.dot_general(x_ref[...], w_ref[...].astype(BF16), contract, preferred_element_type=F32)
        ex = refs[:n_ex]
        outs = refs[n_ex + n_al:n_ex + n_al + n_out]
        scr = refs[n_ex + n_al + n_out:]
        epilogue(acc, x_ref, ex, outs, scr)

    if w_rows_are_outputs:
        assert layer is None
        w_spec = pl.BlockSpec((tn, k), lambda i, j: (cb0 + j, 0))
    elif layer is None:
        w_spec = pl.BlockSpec((k, tn), lambda i, j: (0, cb0 + j))
    else:
        w_spec = pl.BlockSpec((None, k, tn), lambda i, j: (layer, 0, cb0 + j))
    in_specs = [pl.BlockSpec((tm, k), lambda i, j: (i, 0)), w_spec]
    in_specs += list(extra_specs)
    in_specs += [pl.BlockSpec(memory_space=pl.ANY) for _ in alias_inputs]
    aliases = {2 + n_ex + a: out_idx for a, (_, out_idx) in enumerate(alias_inputs)}
    return pl.pallas_call(
        body,
        out_shape=out_shapes,
        grid=(t // tm, n // tn),
        in_specs=in_specs,
        out_specs=out_specs,
        scratch_shapes=list(scratch_shapes),
        input_output_aliases=aliases,
        compiler_params=_cparams("arbitrary", "arbitrary"),
        name=name,
    )(x, w, *extras, *[a for a, _ in alias_inputs])


def _const_spec(shape):
    nd = len(shape)
    return pl.BlockSpec(shape, lambda i, j: (0,) * nd)


def _head_out(nb, sl, tm, nh_t, head_major):
    if head_major:
        nsb = sl // tm
        shape = jax.ShapeDtypeStruct((nb, HEADS, sl, HEAD_DIM), BF16)
        spec = pl.BlockSpec((None, nh_t, tm, HEAD_DIM), lambda i, j: (i // nsb, j, i % nsb, 0))
    else:
        shape = jax.ShapeDtypeStruct((nb * sl, WIDTH), BF16)
        spec = pl.BlockSpec((tm, nh_t * HEAD_DIM), lambda i, j: (i, j))
    return shape, spec


def _store_head(o_ref, hh, val, head_major):
    if head_major:
        o_ref[hh] = val
    else:
        o_ref[:, hh * HEAD_DIM:(hh + 1) * HEAD_DIM] = val


def _proj_heads(h, w, col0, g, scale, nb, sl, head_major, name, w_rows_are_outputs=False):
    t = h.shape[0]
    tm, tn = _row_tile(min(t, sl) if head_major else t, 1024), 512
    nh_t = tn // HEAD_DIM
    shape, spec = _head_out(nb, sl, tm, nh_t, head_major)

    def epi(acc, x_ref, ex, outs, scr):
        g_ref, = ex
        for hh in range(nh_t):
            v = _rms(acc[:, hh * HEAD_DIM:(hh + 1) * HEAD_DIM]) * g_ref[...] * scale
            _store_head(outs[0], hh, v.astype(BF16), head_major)

    return _proj_call(h, w, col0, WIDTH, tm, tn, [g], [_const_spec((1, HEAD_DIM))], [shape], [spec], epi, name,
                      w_rows_are_outputs=w_rows_are_outputs)[0]


def _proj_kv_fox(h, w, col0, g, layer, prev, depth, nb, sl, head_major, name):
    t = h.shape[0]
    tm, tn = _row_tile(min(t, sl) if head_major else t, 512), 512
    nh_t = tn // HEAD_DIM
    f_shape = jax.ShapeDtypeStruct((depth, t * HEADS, HEAD_DIM), F32)
    f_spec = pl.BlockSpec((None, tm * HEADS, HEAD_DIM), lambda i, j: (layer, i, 0))
    shapes, specs = [f_shape], [f_spec]
    if head_major:
        b_shape, b_spec = _head_out(nb, sl, tm, nh_t, True)
        shapes.append(b_shape)
        specs.append(b_spec)
    extras = [] if g is None else [g]
    especs = [] if g is None else [_const_spec((1, HEAD_DIM))]

    def epi(acc, x_ref, ex, outs, scr):
        j = pl.program_id(1)
        for hh in range(nh_t):
            v = acc[:, hh * HEAD_DIM:(hh + 1) * HEAD_DIM]
            if g is not None:
                v = _rms(v) * ex[0][...]
            outs[0][pl.ds(j * nh_t + hh, tm, stride=HEADS), :] = v
            if head_major:
                outs[1][hh] = v.astype(BF16)

    alias = [] if prev is None else [(prev, 0)]
    return _proj_call(h, w, col0, WIDTH, tm, tn, extras, especs, shapes, specs, epi, name, alias_inputs=alias,
                      w_rows_are_outputs=True)


def _proj_gate(h, w, name):
    t = h.shape[0]
    tm, tn = _row_tile(t, 1024), 512

    def epi(acc, x_ref, ex, outs, scr):
        outs[0][...] = _silu(acc).astype(BF16)

    return _proj_call(h, w, COL_GATE, 2 * WIDTH, tm, tn, [], [], [jax.ShapeDtypeStruct((t, 2 * WIDTH), BF16)],
                      [pl.BlockSpec((tm, tn), lambda i, j: (i, j))], epi, name, w_rows_are_outputs=True)[0]


def _proj_cq(h, w, g, name):
    t = h.shape[0]
    tm, tn = _row_tile(t, 1024), Q_LORA

    def epi(acc, x_ref, ex, outs, scr):
        outs[0][...] = (_rms(acc) * ex[0][...]).astype(BF16)

    return _proj_call(h, w, COL_CQ, Q_LORA, tm, tn, [g], [_const_spec((1, Q_LORA))],
                      [jax.ShapeDtypeStruct((t, Q_LORA), BF16)], [pl.BlockSpec((tm, tn), lambda i, j: (i, 0))],
                      epi, name, w_rows_are_outputs=True)[0]


def _proj_ckv(h, w, g, layer, prev, depth, name):
    t = h.shape[0]
    tm, tn = _row_tile(t, 1024), KV_LORA

    def epi(acc, x_ref, ex, outs, scr):
        v = _rms(acc) * ex[0][...]
        outs[0][...] = v
        outs[1][...] = v.astype(BF16)

    shapes = [jax.ShapeDtypeStruct((depth, t, KV_LORA), F32), jax.ShapeDtypeStruct((t, KV_LORA), BF16)]
    specs = [pl.BlockSpec((None, tm, tn), lambda i, j: (layer, i, 0)), pl.BlockSpec((tm, tn), lambda i, j: (i, 0))]
    alias = [] if prev is None else [(prev, 0)]
    return _proj_call(h, w, COL_CKV, KV_LORA, tm, tn, [g], [_const_spec((1, KV_LORA))], shapes, specs, epi, name,
                      alias_inputs=alias, w_rows_are_outputs=True)


def _swap_halves(x):
    lane = lax.broadcasted_iota(jnp.int32, x.shape, 1)
    return jnp.where((lane % ROPE_DIM) < ROPE_DIM // 2, pltpu.roll(x, LANES - ROPE_DIM // 2, 1),
                     pltpu.roll(x, ROPE_DIM // 2, 1))


def _proj_small(h, w, g_kp, b_f_row, cos, sin, layer, prev_kpe, prev_logf, depth, nb, sl, prompt, name):
    t = h.shape[0]
    tm = _row_tile(min(t, sl), 1024) if prompt else t
    nsb = sl // tm if prompt else 1
    seqs_per_blk = 1 if prompt else tm // sl

    def epi(acc, x_ref, ex, outs, scr):
        g_ref, bfr_ref, cos_ref, sin_ref = ex
        kpe_ref, kpe2_ref, logf_ref, cum_ref = outs
        i = pl.program_id(0)
        kp = _rms(acc[:, :LANES], ROPE_DIM) * g_ref[...]
        kp = kp * cos_ref[...] + _swap_halves(kp) * sin_ref[...]
        kpe_ref[...] = kp[:, :ROPE_DIM]
        kpe2_ref[...] = (kp + pltpu.roll(kp, ROPE_DIM, 1)).astype(BF16)
        lf = _log_sigmoid(acc[:, LANES:2 * LANES] + bfr_ref[...])
        logf_ref[...] = lf[:, :HEADS]
        if prompt:
            carry_ref, = scr
            lane = lax.broadcasted_iota(jnp.int32, lf.shape, 1)
            row = lax.broadcasted_iota(jnp.int32, lf.shape, 0)
            c = _scan(jnp.where(lane < HEADS, lf, 0.0), 0, tm, row)
            if nsb > 1:
                @pl.when(i % nsb == 0)
                def _():
                    carry_ref[...] = jnp.zeros_like(carry_ref)
                c = c + carry_ref[0:1, :]
                carry_ref[...] = jnp.broadcast_to(c[tm - 1:tm, :], carry_ref.shape)
            neg = c * (-LOG2E)
            hi = neg.astype(BF16).astype(F32)
            mid = (neg - hi).astype(BF16).astype(F32)
            lo = neg - hi - mid
            cum_ref[...] = (hi + pltpu.roll(mid, HEADS, 1) + pltpu.roll(lo, 2 * HEADS, 1)).astype(BF16)
        else:
            c = lf.T[:HEADS, :]
            pos = lax.broadcasted_iota(jnp.int32, c.shape, 1) % sl
            c = _scan(c, 1, sl, pos) * (-LOG2E)
            cum_ref[...] = jnp.zeros(cum_ref.shape, F32)
            for s in range(seqs_per_blk):
                cum_ref[s, :, 0:sl] = c[:, s * sl:(s + 1) * sl]

    if prompt:
        cum_shape = jax.ShapeDtypeStruct((t, LANES), BF16)
        cum_spec = pl.BlockSpec((tm, LANES), lambda i, j: (i, 0))
        scratch = [pltpu.VMEM((SUBLANES, LANES), F32)]
    else:
        assert sl <= LANES
        cum_shape = jax.ShapeDtypeStruct((nb, HEADS, LANES), F32)
        cum_spec = pl.BlockSpec((seqs_per_blk, HEADS, LANES), lambda i, j: (i, 0, 0))
        scratch = []
    shapes = [jax.ShapeDtypeStruct((depth, t, ROPE_DIM), F32), jax.ShapeDtypeStruct((t, LANES), BF16),
              jax.ShapeDtypeStruct((depth, t, HEADS), F32), cum_shape]
    specs = [pl.BlockSpec((None, tm, ROPE_DIM), lambda i, j: (layer, i, 0)),
             pl.BlockSpec((tm, LANES), lambda i, j: (i, 0)),
             pl.BlockSpec((None, tm, HEADS), lambda i, j: (layer, i, 0)), cum_spec]
    extras = [g_kp, b_f_row, cos, sin]
    especs = [_const_spec((1, LANES)), _const_spec((1, LANES)),
              pl.BlockSpec((tm, LANES), lambda i, j: (i, 0)), pl.BlockSpec((tm, LANES), lambda i, j: (i, 0))]
    alias = []
    if prev_kpe is not None:
        alias = [(prev_kpe, 0), (prev_logf, 2)]
    return _proj_call(h, w, COL_SMALL, N_SMALL, tm, N_SMALL, extras, especs, shapes, specs, epi, name,
                      alias_inputs=alias, scratch_shapes=scratch, w_rows_are_outputs=True)


def _proj_q_pe(cq, w, g2, cos, sin, nb, sl, head_major, name):
    t = cq.shape[0]
    tm, tn = _row_tile(min(t, sl) if head_major else t, 512), HEADS * ROPE_DIM
    shape, spec = _head_out(nb, sl, tm, HEADS, head_major)

    def epi(acc, x_ref, ex, outs, scr):
        g_ref, cos_ref, sin_ref = ex
        lane = lax.broadcasted_iota(jnp.int32, (tm, LANES), 1)
        lo = lane < ROPE_DIM
        for p in range(HEADS // 2):
            x = acc[:, p * LANES:(p + 1) * LANES]
            x2 = x * x
            ms_lo = jnp.sum(jnp.where(lo, x2, 0.0), axis=-1, keepdims=True)
            ms_hi = jnp.sum(jnp.where(lo, 0.0, x2), axis=-1, keepdims=True)
            ms = jnp.where(lo, ms_lo, ms_hi) * (1.0 / ROPE_DIM)
            y = x * lax.rsqrt(ms + EPS) * g_ref[...]
            y = (y * cos_ref[...] + _swap_halves(y) * sin_ref[...]) * MLA_SCALE
            _store_head(outs[0], 2 * p, jnp.where(lo, y, 0.0).astype(BF16), head_major)
            _store_head(outs[0], 2 * p + 1, jnp.where(lo, 0.0, y).astype(BF16), head_major)

    especs = [_const_spec((1, LANES)), pl.BlockSpec((tm, LANES), lambda i, j: (i, 0)),
              pl.BlockSpec((tm, LANES), lambda i, j: (i, 0))]
    return _proj_call(cq, w, WIDTH, tn, tm, tn, [g2, cos, sin], especs, [shape], [spec], epi, name)[0]


def _proj_kv_up(ckv, w, layer, g, nb, sl, name):
    t = ckv.shape[0]
    tm, tn = _row_tile(min(t, sl), 1024), 512
    nh_t = tn // (2 * HEAD_DIM)
    nsb = sl // tm
    shape = jax.ShapeDtypeStruct((nb, HEADS, sl, HEAD_DIM), BF16)
    spec = pl.BlockSpec((None, nh_t, tm, HEAD_DIM), lambda i, j: (i // nsb, j, i % nsb, 0))

    def epi(acc, x_ref, ex, outs, scr):
        for hh in range(nh_t):
            base = hh * 2 * HEAD_DIM
            outs[0][hh] = (_rms(acc[:, base:base + HEAD_DIM]) * ex[0][...]).astype(BF16)
            outs[1][hh] = acc[:, base + HEAD_DIM:base + 2 * HEAD_DIM].astype(BF16)

    return _proj_call(ckv, w, 0, 2 * WIDTH, tm, tn, [g], [_const_spec((1, HEAD_DIM))], [shape, shape],
                      [spec, spec], epi, name, layer=layer)


def _flash_call(q_parts, k_head, k_shared, v, gate, gate_col0, nb, sl, fox, name):
    tq = _row_tile(sl, 512)
    nqb = sl // tq
    t = nb * sl
    n_q = len(q_parts)

    def body(*refs):
        q_refs = refs[:n_q]
        kh_ref, ks_ref, v_ref, g_ref, o_ref = refs[n_q:]
        h = pl.program_id(1)
        key = lax.broadcasted_iota(jnp.int32, (tq, tq), 0)
        qry = lax.broadcasted_iota(jnp.int32, (tq, tq), 1)
        allowed = (key <= qry) if fox else (key // CHUNK <= qry // CHUNK)
        if fox:
            lane = lax.broadcasted_iota(jnp.int32, (tq, LANES), 1)
            selector = jnp.where((lane % HEADS == h) & (lane < 3 * HEADS), 1.0, 0.0).astype(BF16)

        def q_block(qi):
            rows = slice(qi * tq, (qi + 1) * tq)
            parts = [r[rows, :] for r in q_refs] + ([selector] if fox else [])
            return jnp.concatenate(parts, axis=-1)

        def k_block(kb):
            rows = slice(kb * tq, (kb + 1) * tq)
            return jnp.concatenate([kh_ref[rows, :], ks_ref[rows, :]], axis=-1)

        def scores(rnd):
            out = {}
            for qi in range(rnd, nqb):
                s = lax.dot_general(k_block(qi - rnd), q_block(qi), (((1,), (1,)), ((), ())),
                                    preferred_element_type=F32)
                out[qi] = jnp.where(allowed, s, NEG_INF) if rnd == 0 else s
            return out

        state = {qi: (jnp.full((1, tq), NEG_INF, F32), jnp.zeros((1, tq), F32), jnp.zeros((HEAD_DIM, tq), F32))
                 for qi in range(nqb)}
        s_cur = scores(0)
        for rnd in range(nqb):
            s_next = scores(rnd + 1) if rnd + 1 < nqb else {}
            for qi in range(rnd, nqb):
                m, l, acc = state[qi]
                s = s_cur[qi]
                m_new = jnp.maximum(m, jnp.max(s, axis=0, keepdims=True))
                alpha = jnp.exp2(m - m_new)
                p = jnp.exp2(s - m_new)
                l = alpha * l + jnp.sum(p, axis=0, keepdims=True)
                kb = qi - rnd
                pv = lax.dot_general(v_ref[kb * tq:(kb + 1) * tq, :], p.astype(BF16), (((0,), (0,)), ((), ())),
                                     preferred_element_type=F32)
                state[qi] = (m_new, l, alpha * acc + pv)
            s_cur = s_next
        for qi in range(nqb):
            m, l, acc = state[qi]
            rows = slice(qi * tq, (qi + 1) * tq)
            o_ref[rows, :] = ((acc / l).T * g_ref[rows, :].astype(F32)).astype(BF16)

    head_spec = pl.BlockSpec((None, None, sl, HEAD_DIM), lambda b, h: (b, h, 0, 0))
    in_specs = [head_spec for _ in q_parts] + [
        head_spec, pl.BlockSpec((None, sl, LANES), lambda b, h: (b, 0, 0)), head_spec,
        pl.BlockSpec((None, sl, HEAD_DIM), lambda b, h: (b, 0, gate_col0 + h))]
    return pl.pallas_call(
        body,
        out_shape=jax.ShapeDtypeStruct((nb, sl, WIDTH), BF16),
        grid=(nb, HEADS),
        in_specs=in_specs,
        out_specs=pl.BlockSpec((None, sl, HEAD_DIM), lambda b, h: (b, 0, h)),
        compiler_params=_cparams("parallel", "parallel"),
        name=name,
    )(*q_parts, k_head, k_shared, v, gate.reshape(nb, sl, 2 * WIDTH)).reshape(t, WIDTH)


def _pad_rows(x, rows):
    return jnp.concatenate([x, jnp.zeros((rows - x.shape[0], x.shape[1]), x.dtype)], axis=0)


def _fox_decode_call(q, cache_k, cache_v, new_k, new_v, layer, logf_t, cum_new, gate, nb, sd, past, name):
    tk = _row_tile(past, 512)
    nkc = past // tk
    ts = nb * sd
    rows = HEADS * sd

    def body(q_ref, ck_ref, cv_ref, nk_ref, nv_ref, lf_ref, cn_ref, g_ref, o_ref, cum_ref, m_ref, l_ref, acc_ref):
        kc = pl.program_id(1)

        @pl.when(kc == 0)
        def _():
            c = lf_ref[...]
            cum_ref[...] = _scan(c, 1, past, lax.broadcasted_iota(jnp.int32, c.shape, 1)) * (-LOG2E)
            m_ref[...] = jnp.full(m_ref.shape, NEG_INF, F32)
            l_ref[...] = jnp.zeros(l_ref.shape, F32)
            acc_ref[...] = jnp.zeros(acc_ref.shape, F32)

        def attend(keys, values, bias, mask):
            s = []
            for hh in range(HEADS):
                qh = q_ref[:, hh * HEAD_DIM:(hh + 1) * HEAD_DIM]
                sh = lax.dot_general(qh, keys(hh), (((1,), (1,)), ((), ())), preferred_element_type=F32)
                sh = sh + bias(hh)
                s.append(sh if mask is None else jnp.where(mask, sh, NEG_INF))
            s = jnp.concatenate(s, axis=0)
            m_old = m_ref[...]
            m_new = jnp.maximum(m_old, jnp.max(s, axis=-1, keepdims=True))
            alpha = jnp.exp2(m_old - m_new)
            p = jnp.exp2(s - m_new)
            l_ref[...] = alpha * l_ref[...] + jnp.sum(p, axis=-1, keepdims=True)
            m_ref[...] = m_new
            pb = p.astype(BF16)
            pv = [jnp.dot(pb[hh * sd:(hh + 1) * sd, :], values(hh), preferred_element_type=F32)
                  for hh in range(HEADS)]
            acc_ref[...] = alpha * acc_ref[...] + jnp.concatenate(pv, axis=0)

        def by_head(ref, n):
            return jnp.swapaxes(ref[...].astype(BF16).reshape(n, HEADS, HEAD_DIM), 0, 1)

        start = pl.multiple_of(kc * tk, tk)
        kk, vv = by_head(ck_ref, tk), by_head(cv_ref, tk)
        attend(lambda hh: kk[hh], lambda hh: vv[hh], lambda hh: cum_ref[hh:hh + 1, pl.ds(start, tk)], None)

        @pl.when(kc == nkc - 1)
        def _():
            row = lax.broadcasted_iota(jnp.int32, (sd, NEW_PAD), 0)
            col = lax.broadcasted_iota(jnp.int32, (sd, NEW_PAD), 1)
            nk, nv = by_head(nk_ref, sd), by_head(nv_ref, sd)
            attend(lambda hh: _pad_rows(nk[hh], NEW_PAD), lambda hh: _pad_rows(nv[hh], NEW_PAD),
                   lambda hh: cum_ref[hh:hh + 1, past - 1:past] + cn_ref[hh:hh + 1, :], col <= row)
            out = acc_ref[...] / l_ref[...]
            for hh in range(HEADS):
                gh = g_ref[:, hh * HEAD_DIM:(hh + 1) * HEAD_DIM].astype(F32)
                o_ref[:, hh * HEAD_DIM:(hh + 1) * HEAD_DIM] = (out[hh * sd:(hh + 1) * sd, :] * gh).astype(BF16)

    cache_spec = pl.BlockSpec((tk * HEADS, HEAD_DIM), lambda b, kc: ((layer * nb + b) * nkc + kc, 0))
    new_spec = pl.BlockSpec((None, sd * HEADS, HEAD_DIM), lambda b, kc: (layer, b, 0))
    return pl.pallas_call(
        body,
        out_shape=jax.ShapeDtypeStruct((ts, WIDTH), BF16),
        grid=(nb, nkc),
        in_specs=[
            pl.BlockSpec((sd, WIDTH), lambda b, kc: (b, 0)),
            cache_spec, cache_spec, new_spec, new_spec,
            pl.BlockSpec((None, None, HEADS, past), lambda b, kc: (layer, b, 0, 0)),
            pl.BlockSpec((None, HEADS, NEW_PAD), lambda b, kc: (b, 0, 0)),
            pl.BlockSpec((sd, WIDTH), lambda b, kc: (b, 0)),
        ],
        out_specs=pl.BlockSpec((sd, WIDTH), lambda b, kc: (b, 0)),
        scratch_shapes=[pltpu.VMEM((HEADS, past), F32), pltpu.VMEM((rows, 1), F32),
                        pltpu.VMEM((rows, 1), F32), pltpu.VMEM((rows, HEAD_DIM), F32)],
        compiler_params=_cparams("parallel", "arbitrary"),
        name=name,
    )(q, cache_k, cache_v, new_k, new_v, logf_t, cum_new, gate)


MLA_DECODE_HEADS_PER_STEP = 2


def _mla_decode_call(q_nope, q_pe, cache_ckv, cache_kpe, new_ckv, new_kpe2, layer, w_kvb, g_kn, gate,
                     nb, sd, past, name):
    ts = nb * sd
    rows = past + NEW_PAD
    hps = MLA_DECODE_HEADS_PER_STEP
    wblk = hps * HEAD_DIM

    def body(qn_ref, qp_ref, cc_ref, ck_ref, nc_ref, nk_ref, w_ref, g_ref, gate_ref, o_ref, lat_ref, kpe_ref):
        hp = pl.program_id(1)

        @pl.when(hp == 0)
        def _():
            lat_ref[0:past, :] = cc_ref[...].astype(BF16)
            lat_ref[past:rows, :] = _pad_rows(nc_ref[...], NEW_PAD)
            kp = ck_ref[...].astype(BF16)
            kpe_ref[0:past, :] = jnp.concatenate([kp, kp], axis=-1)
            kpe_ref[past:rows, :] = _pad_rows(nk_ref[...], NEW_PAD)

        kv = jnp.dot(lat_ref[...], w_ref[...].astype(BF16), preferred_element_type=F32)
        q_pos = past + lax.broadcasted_iota(jnp.int32, (sd, rows), 0)
        k_pos = lax.broadcasted_iota(jnp.int32, (sd, rows), 1)
        allowed = (k_pos // CHUNK <= q_pos // CHUNK) & (k_pos < past + sd)
        for g in range(hps):
            base = g * 2 * HEAD_DIM
            kn = (_rms(kv[:, base:base + HEAD_DIM]) * g_ref[...]).astype(BF16)
            vv = kv[:, base + HEAD_DIM:base + 2 * HEAD_DIM].astype(BF16)
            k = jnp.concatenate([kn, kpe_ref[...]], axis=-1)
            hs = slice(g * HEAD_DIM, (g + 1) * HEAD_DIM)
            q = jnp.concatenate([qn_ref[:, hs], qp_ref[:, hs]], axis=-1)
            s = lax.dot_general(q, k, (((1,), (1,)), ((), ())), preferred_element_type=F32)
            s = jnp.where(allowed, s, NEG_INF)
            m = jnp.max(s, axis=-1, keepdims=True)
            p = jnp.exp2(s - m)
            l = jnp.sum(p, axis=-1, keepdims=True)
            out = jnp.dot(p.astype(BF16), vv, preferred_element_type=F32) / l
            o_ref[:, hs] = (out * gate_ref[:, hs].astype(F32)).astype(BF16)

    return pl.pallas_call(
        body,
        out_shape=jax.ShapeDtypeStruct((ts, WIDTH), BF16),
        grid=(nb, HEADS // hps),
        in_specs=[
            pl.BlockSpec((sd, wblk), lambda b, h: (b, h)),
            pl.BlockSpec((sd, wblk), lambda b, h: (b, h)),
            pl.BlockSpec((None, past, KV_LORA), lambda b, h: (layer * nb + b, 0, 0)),
            pl.BlockSpec((None, past, ROPE_DIM), lambda b, h: (layer * nb + b, 0, 0)),
            pl.BlockSpec((sd, KV_LORA), lambda b, h: (b, 0)),
            pl.BlockSpec((sd, LANES), lambda b, h: (b, 0)),
            pl.BlockSpec((None, KV_LORA, 2 * wblk), lambda b, h: (layer, 0, h)),
            pl.BlockSpec((1, HEAD_DIM), lambda b, h: (0, 0)),
            pl.BlockSpec((sd, wblk), lambda b, h: (b, HEADS // hps + h)),
        ],
        out_specs=pl.BlockSpec((sd, wblk), lambda b, h: (b, h)),
        scratch_shapes=[pltpu.VMEM((rows, KV_LORA), BF16), pltpu.VMEM((rows, LANES), BF16)],
        compiler_params=_cparams("parallel", "arbitrary"),
        name=name,
    )(q_nope, q_pe, cache_ckv, cache_kpe, new_ckv, new_kpe2, w_kvb, g_kn, gate)


def _out_proj_call(o_a, o_b, w_out, layer, x, name):
    t, d = x.shape
    tm, tn = _row_tile(t, 1024), 512

    def body(a_ref, b_ref, wa_ref, wb_ref, x_ref, y_ref):
        acc = jnp.dot(a_ref[...], wa_ref[...].astype(BF16), preferred_element_type=F32)
        acc = acc + jnp.dot(b_ref[...], wb_ref[...].astype(BF16), preferred_element_type=F32)
        y_ref[...] = x_ref[...] + acc

    return pl.pallas_call(
        body,
        out_shape=jax.ShapeDtypeStruct((t, d), F32),
        grid=(t // tm, d // tn),
        in_specs=[
            pl.BlockSpec((tm, WIDTH), lambda i, j: (i, 0)),
            pl.BlockSpec((tm, WIDTH), lambda i, j: (i, 0)),
            pl.BlockSpec((None, WIDTH, tn), lambda i, j: (layer, 0, j)),
            pl.BlockSpec((None, WIDTH, tn), lambda i, j: (layer, 1, j)),
            pl.BlockSpec((tm, tn), lambda i, j: (i, j)),
        ],
        out_specs=pl.BlockSpec((tm, tn), lambda i, j: (i, j)),
        compiler_params=_cparams("parallel", "parallel"),
        name=name,
    )(o_a, o_b, w_out, w_out, x)


def _rope_tables(pos):
    half = ROPE_DIM // 2
    inv_freq = 1.0 / (ROPE_THETA ** (jnp.arange(half, dtype=F32) / half))
    ang = pos.astype(F32)[:, None] * inv_freq[None, :]
    cos, sin = jnp.cos(ang), jnp.sin(ang)
    cos64 = jnp.concatenate([cos, cos], axis=-1)
    sin64 = jnp.concatenate([-sin, sin], axis=-1)
    return jnp.concatenate([cos64, cos64], axis=-1), jnp.concatenate([sin64, sin64], axis=-1)


def _prep_w_in_call(w_in_t, layer, name):
    _, n_in, d = w_in_t.shape
    tc = _row_tile(d, 256)
    src = {}
    o = 0
    for seg, size in (("q", WIDTH), ("k", WIDTH), ("v", WIDTH), ("f", HEADS), ("za", WIDTH),
                      ("cq", Q_LORA), ("ckv", KV_LORA), ("kpe", ROPE_DIM), ("zb", WIDTH)):
        src[seg] = (o, size)
        o += size
    assert o == n_in
    dst = (("q", COL_Q), ("k", COL_K), ("v", COL_V), ("za", COL_GATE), ("zb", COL_GATE + WIDTH), ("cq", COL_CQ),
           ("ckv", COL_CKV), ("kpe", COL_SMALL), ("f", COL_SMALL + LANES))

    def body(x_ref, o_ref):
        for seg, d0 in dst:
            s0, size = src[seg]
            o_ref[d0:d0 + size, :] = x_ref[s0:s0 + size, :].astype(BF16)
        pad0 = COL_SMALL + ROPE_DIM
        o_ref[pad0:COL_SMALL + LANES, :] = jnp.zeros((LANES - ROPE_DIM, tc), BF16)
        pad1 = COL_SMALL + LANES + HEADS
        o_ref[pad1:COL_SMALL + N_SMALL, :] = jnp.zeros((LANES - HEADS, tc), BF16)

    return pl.pallas_call(
        body,
        out_shape=jax.ShapeDtypeStruct((COL_SMALL + N_SMALL, d), BF16),
        grid=(d // tc,),
        in_specs=[pl.BlockSpec((None, n_in, tc), lambda i: (layer, 0, i))],
        out_specs=pl.BlockSpec((COL_SMALL + N_SMALL, tc), lambda i: (0, i)),
        compiler_params=_cparams("parallel"),
        name=name,
    )(w_in_t)


def _prep_w_qb_call(w_qb, layer, name):
    _, k, n = w_qb.shape
    per_head = NOPE_DIM + ROPE_DIM
    tr = _row_tile(k, 256)

    def body(x_ref, o_ref):
        for hh in range(HEADS):
            o_ref[:, hh * NOPE_DIM:(hh + 1) * NOPE_DIM] = x_ref[:, hh * per_head:hh * per_head + NOPE_DIM].astype(BF16)
        for p in range(HEADS // 2):
            pe = [x_ref[:, hh * per_head + NOPE_DIM:(hh + 1) * per_head] for hh in (2 * p, 2 * p + 1)]
            o_ref[:, WIDTH + p * LANES:WIDTH + (p + 1) * LANES] = jnp.concatenate(pe, axis=1).astype(BF16)

    return pl.pallas_call(
        body,
        out_shape=jax.ShapeDtypeStruct((k, n), BF16),
        grid=(k // tr,),
        in_specs=[pl.BlockSpec((None, tr, n), lambda i: (layer, i, 0))],
        out_specs=pl.BlockSpec((tr, n), lambda i: (i, 0)),
        compiler_params=_cparams("parallel"),
        name=name,
    )(w_qb)


def _layer_params(l, g_norm, b_f, g_q_fox, g_k_fox, g_cq, g_qn, g_qp, g_ckv, g_kp, g_kn):
    row = lambda a: a.reshape(1, -1).astype(F32)
    pad = lambda a: jnp.concatenate([row(a), jnp.zeros((1, LANES - a.shape[-1]), F32)], axis=1)
    return dict(
        g_norm=row(g_norm[l]), b_f_row=pad(b_f[l]), g_q=row(g_q_fox[l]), g_k=row(g_k_fox[l]), g_cq=row(g_cq[l]),
        g_qn=row(g_qn[l]), g_qp2=jnp.concatenate([row(g_qp[l]), row(g_qp[l])], axis=1),
        g_ckv=row(g_ckv[l]), g_kp=pad(g_kp[l]), g_kn=row(g_kn[l]),
    )


def _input_stage(x, w, w_qb, p, cos, sin, layer, prev, depth, nb, sl, prompt, tag):
    h = _rmsnorm_call(x, p["g_norm"], f"norm_{tag}")
    prev = prev or {}
    r = {}
    r["q"] = _proj_heads(h, w, COL_Q, p["g_q"], FOX_SCALE, nb, sl, prompt, f"proj_q_{tag}",
                         w_rows_are_outputs=True)
    k_out = _proj_kv_fox(h, w, COL_K, p["g_k"], layer, prev.get("k"), depth, nb, sl, prompt, f"proj_k_{tag}")
    v_out = _proj_kv_fox(h, w, COL_V, None, layer, prev.get("v"), depth, nb, sl, prompt, f"proj_v_{tag}")
    r["k"], r["v"] = k_out[0], v_out[0]
    if prompt:
        r["k_bf"], r["v_bf"] = k_out[1], v_out[1]
    r["gate"] = _proj_gate(h, w, f"proj_gate_{tag}")
    cq = _proj_cq(h, w, p["g_cq"], f"proj_cq_{tag}")
    r["ckv"], r["ckv_bf"] = _proj_ckv(h, w, p["g_ckv"], layer, prev.get("ckv"), depth, f"proj_ckv_{tag}")
    r["kpe"], r["kpe2"], r["logf"], r["cum"] = _proj_small(
        h, w, p["g_kp"], p["b_f_row"], cos, sin, layer, prev.get("kpe"), prev.get("logf"), depth, nb, sl, prompt,
        f"proj_small_{tag}")
    r["q_nope"] = _proj_heads(cq, w_qb, 0, p["g_qn"], MLA_SCALE, nb, sl, prompt, f"proj_qn_{tag}")
    r["q_pe"] = _proj_q_pe(cq, w_qb, p["g_qp2"], cos, sin, nb, sl, prompt, f"proj_qp_{tag}")
    return r


def kernel(x_prompt, x_sample, cache_fox_k, cache_fox_v, cache_fox_logf, cache_mla_ckv, cache_mla_kpe, g_norm,
           w_in, b_f, g_q_fox, g_k_fox, g_cq, w_qb, g_qn, g_qp, g_ckv, g_kp, w_kvb, g_kn, w_out):
    nb, sl, d = x_prompt.shape
    nbd, sd, _ = x_sample.shape
    depth = w_in.shape[0]
    past = cache_fox_k.shape[2]
    assert sl % CHUNK == 0 and past % CHUNK == 0 and sd <= CHUNK, "chunk-aligned streaming shapes only"

    cos_p, sin_p = _rope_tables(jnp.arange(sl, dtype=jnp.int32))
    cos_p, sin_p = jnp.tile(cos_p, (nb, 1)), jnp.tile(sin_p, (nb, 1))
    cos_s, sin_s = _rope_tables(past + jnp.arange(sd, dtype=jnp.int32))
    cos_s, sin_s = jnp.tile(cos_s, (nbd, 1)), jnp.tile(sin_s, (nbd, 1))

    ck2 = cache_fox_k.reshape(-1, HEAD_DIM)
    cv2 = cache_fox_v.reshape(-1, HEAD_DIM)
    logf_t = jnp.swapaxes(cache_fox_logf, 2, 3)
    w_in_t = jnp.swapaxes(w_in, 1, 2)
    cckv = cache_mla_ckv.reshape(depth * nbd, past, KV_LORA)
    ckpe = cache_mla_kpe.reshape(depth * nbd, past, ROPE_DIM)

    y_p = x_prompt.reshape(nb * sl, d)
    y_s = x_sample.reshape(nbd * sd, d)
    prev_p, prev_s = None, None
    for l in range(depth):
        p = _layer_params(l, g_norm, b_f, g_q_fox, g_k_fox, g_cq, g_qn, g_qp, g_ckv, g_kp, g_kn)
        w = _prep_w_in_call(w_in_t, l, f"prep_w_in_{l}")
        wq = _prep_w_qb_call(w_qb, l, f"prep_w_qb_{l}")
        rp = _input_stage(y_p, w, wq, p, cos_p, sin_p, l, prev_p, depth, nb, sl, True, f"p{l}")
        kn_p, v_p = _proj_kv_up(rp["ckv_bf"], w_kvb, l, p["g_kn"], nb, sl, f"kv_up_p{l}")
        o_a = _flash_call([rp["q"]], rp["k_bf"], rp["cum"].reshape(nb, sl, LANES), rp["v_bf"], rp["gate"], 0,
                          nb, sl, True, f"fox_attn_p{l}")
        o_b = _flash_call([rp["q_nope"], rp["q_pe"]], kn_p, rp["kpe2"].reshape(nb, sl, LANES), v_p, rp["gate"],
                          HEADS, nb, sl, False, f"mla_attn_p{l}")
        y_p = _out_proj_call(o_a, o_b, w_out, l, y_p, f"out_proj_p{l}")
        prev_p = rp
        rs = _input_stage(y_s, w, wq, p, cos_s, sin_s, l, prev_s, depth, nbd, sd, False, f"s{l}")
        o_a = _fox_decode_call(rs["q"], ck2, cv2, rs["k"], rs["v"], l, logf_t, rs["cum"], rs["gate"], nbd, sd,
                               past, f"fox_attn_s{l}")
        o_b = _mla_decode_call(rs["q_nope"], rs["q_pe"], cckv, ckpe, rs["ckv_bf"], rs["kpe2"], l, w_kvb,
                               p["g_kn"], rs["gate"], nbd, sd, past, f"mla_attn_s{l}")
        y_s = _out_proj_call(o_a, o_b, w_out, l, y_s, f"out_proj_s{l}")
        prev_s = rs

    hk = (HEADS, HEAD_DIM)
    return (y_p.reshape(nb, sl, d), y_s.reshape(nbd, sd, d),
            prev_p["k"].reshape((depth, nb, sl) + hk), prev_p["v"].reshape((depth, nb, sl) + hk),
            prev_p["logf"].reshape(depth, nb, sl, HEADS), prev_p["ckv"].reshape(depth, nb, sl, KV_LORA),
            prev_p["kpe"].reshape(depth, nb, sl, ROPE_DIM),
            prev_s["k"].reshape((depth, nbd, sd) + hk), prev_s["v"].reshape((depth, nbd, sd) + hk),
            prev_s["logf"].reshape(depth, nbd, sd, HEADS), prev_s["ckv"].reshape(depth, nbd, sd, KV_LORA),
            prev_s["kpe"].reshape(depth, nbd, sd, ROPE_DIM))
```

```python
import math

import jax
import jax.numpy as jnp
from jax import lax
from jax.experimental import pallas as pl
from jax.experimental.pallas import tpu as pltpu

F32 = jnp.float32
BF16 = jnp.bfloat16

CHUNK = 64
HEADS = 16
HEAD_DIM = 128
WIDTH = HEADS * HEAD_DIM
NOPE_DIM = 128
ROPE_DIM = 64
Q_LORA = 1024
KV_LORA = 512
ROPE_THETA = 10000.0
EPS = 1e-6
LOG2E = math.log2(math.e)
FOX_SCALE = LOG2E / math.sqrt(HEAD_DIM)
MLA_SCALE = LOG2E / math.sqrt(NOPE_DIM + ROPE_DIM)
LANES = 128
SUBLANES = 8
NEW_PAD = 128
VMEM_LIMIT_BYTES = 52 * 1024 * 1024
NEG_INF = float("-inf")

COL_Q, COL_K, COL_V, COL_GATE = 0, WIDTH, 2 * WIDTH, 3 * WIDTH
COL_CQ = 5 * WIDTH
COL_CKV = COL_CQ + Q_LORA
COL_SMALL = COL_CKV + KV_LORA
N_SMALL = 2 * LANES


def _cparams(*sem):
    return pltpu.CompilerParams(dimension_semantics=sem, vmem_limit_bytes=VMEM_LIMIT_BYTES)


def _row_tile(t, cap):
    if t <= cap:
        return t
    for c in range(cap, 15, -16):
        if t % c == 0:
            return c
    return t


def _rms(x, n=None, axis=-1):
    n = x.shape[axis] if n is None else n
    ms = jnp.sum(x * x, axis=axis, keepdims=True) * (1.0 / n)
    return x * lax.rsqrt(ms + EPS)


def _silu(x):
    return x * (1.0 / (1.0 + jnp.exp(-x)))


def _log_sigmoid(x):
    return jnp.minimum(x, 0.0) - jnp.log1p(jnp.exp(-jnp.abs(x)))


def _scan(x, axis, length, pos):
    sh = 1
    while sh < length:
        x = x + jnp.where(pos >= sh, pltpu.roll(x, sh, axis), 0.0)
        sh *= 2
    return x


def _rmsnorm_call(x, g, name):
    t, d = x.shape
    tm = _row_tile(t, 256)

    def body(x_ref, g_ref, o_ref):
        o_ref[...] = (_rms(x_ref[...]) * g_ref[...]).astype(BF16)

    return pl.pallas_call(
        body,
        out_shape=jax.ShapeDtypeStruct((t, d), BF16),
        grid=(t // tm,),
        in_specs=[pl.BlockSpec((tm, d), lambda i: (i, 0)), pl.BlockSpec((1, d), lambda i: (0, 0))],
        out_specs=pl.BlockSpec((tm, d), lambda i: (i, 0)),
        compiler_params=_cparams("parallel"),
        name=name,
    )(x, g)


def _proj_call(x, w, col0, n, tm, tn, extras, extra_specs, out_shapes, out_specs, epilogue, name,
               alias_inputs=(), scratch_shapes=(), layer=None, w_rows_are_outputs=False, row_chunks=1):
    t, k = x.shape
    assert col0 % tn == 0 and n % tn == 0 and t % tm == 0
    cb0 = col0 // tn
    n_ex, n_al, n_out = len(extras), len(alias_inputs), len(out_shapes)
    contract = (((1,), (1,)), ((), ())) if w_rows_are_outputs else (((1,), (0,)), ((), ()))

    assert tm % (16 * row_chunks) == 0
    nr = tm // row_chunks

    def body(x_ref, w_ref, *refs):
        ex = refs[:n_ex]
        outs = refs[n_ex + n_al:n_ex + n_al + n_out]
        scr = refs[n_ex + n_al + n_out:]
        wv = w_ref[...].astype(BF16)
        for c in range(row_chunks):
            rows = slice(c * nr, (c + 1) * nr)
            acc = lax.dot_general(x_ref[rows, :], wv, contract, preferred_element_type=F32)
            epilogue(acc, rows, x_ref, ex, outs, scr)

    if w_rows_are_outputs:
        assert layer is None
        w_spec = pl.BlockSpec((tn, k), lambda i, j: (cb0 + j, 0))
    elif layer is None:
        w_spec = pl.BlockSpec((k, tn), lambda i, j: (0, cb0 + j))
    else:
        w_spec = pl.BlockSpec((None, k, tn), lambda i, j: (layer, 0, cb0 + j))
    in_specs = [pl.BlockSpec((tm, k), lambda i, j: (i, 0)), w_spec]
    in_specs += list(extra_specs)
    in_specs += [pl.BlockSpec(memory_space=pl.ANY) for _ in alias_inputs]
    aliases = {2 + n_ex + a: out_idx for a, (_, out_idx) in enumerate(alias_inputs)}
    return pl.pallas_call(
        body,
        out_shape=out_shapes,
        grid=(t // tm, n // tn),
        in_specs=in_specs,
        out_specs=out_specs,
        scratch_shapes=list(scratch_shapes),
        input_output_aliases=aliases,
        compiler_params=_cparams("arbitrary", "arbitrary"),
        name=name,
    )(x, w, *extras, *[a for a, _ in alias_inputs])


def _chunks(tm):
    return 2 if tm % 512 == 0 else 1


def _const_spec(shape):
    nd = len(shape)
    return pl.BlockSpec(shape, lambda i, j: (0,) * nd)


def _head_out(nb, sl, tm, nh_t, head_major):
    if head_major:
        nsb = sl // tm
        shape = jax.ShapeDtypeStruct((nb, HEADS, sl, HEAD_DIM), BF16)
        spec = pl.BlockSpec((None, nh_t, tm, HEAD_DIM), lambda i, j: (i // nsb, j, i % nsb, 0))
    else:
        shape = jax.ShapeDtypeStruct((nb * sl, WIDTH), BF16)
        spec = pl.BlockSpec((tm, nh_t * HEAD_DIM), lambda i, j: (i, j))
    return shape, spec


def _store_head(o_ref, hh, rows, val, head_major):
    if head_major:
        o_ref[hh, rows, :] = val
    else:
        o_ref[rows, hh * HEAD_DIM:(hh + 1) * HEAD_DIM] = val


def _proj_heads(h, w, col0, g, scale, nb, sl, head_major, name, w_rows_are_outputs=False):
    t = h.shape[0]
    tm, tn = _row_tile(min(t, sl) if head_major else t, 1024), 512
    nh_t = tn // HEAD_DIM
    shape, spec = _head_out(nb, sl, tm, nh_t, head_major)

    def epi(acc, rows, x_ref, ex, outs, scr):
        g_ref, = ex
        for hh in range(nh_t):
            v = _rms(acc[:, hh * HEAD_DIM:(hh + 1) * HEAD_DIM]) * g_ref[...] * scale
            _store_head(outs[0], hh, rows, v.astype(BF16), head_major)

    return _proj_call(h, w, col0, WIDTH, tm, tn, [g], [_const_spec((1, HEAD_DIM))], [shape], [spec], epi, name,
                      w_rows_are_outputs=w_rows_are_outputs, row_chunks=_chunks(tm))[0]


def _proj_kv_fox(h, w, col0, g, layer, prev, depth, nb, sl, head_major, name):
    t = h.shape[0]
    tm, tn = _row_tile(min(t, sl) if head_major else t, 1024), 512
    nh_t = tn // HEAD_DIM
    f_shape = jax.ShapeDtypeStruct((depth, t * HEADS, HEAD_DIM), F32)
    f_spec = pl.BlockSpec((None, tm * HEADS, HEAD_DIM), lambda i, j: (layer, i, 0))
    shapes, specs = [f_shape], [f_spec]
    if head_major:
        b_shape, b_spec = _head_out(nb, sl, tm, nh_t, True)
        shapes.append(b_shape)
        specs.append(b_spec)
    extras = [] if g is None else [g]
    especs = [] if g is None else [_const_spec((1, HEAD_DIM))]

    def epi(acc, rows, x_ref, ex, outs, scr):
        j = pl.program_id(1)
        for hh in range(nh_t):
            v = acc[:, hh * HEAD_DIM:(hh + 1) * HEAD_DIM]
            if g is not None:
                v = _rms(v) * ex[0][...]
            nr = rows.stop - rows.start
            outs[0][pl.ds(rows.start * HEADS + j * nh_t + hh, nr, stride=HEADS), :] = v
            if head_major:
                outs[1][hh, rows, :] = v.astype(BF16)

    alias = [] if prev is None else [(prev, 0)]
    return _proj_call(h, w, col0, WIDTH, tm, tn, extras, especs, shapes, specs, epi, name, alias_inputs=alias,
                      w_rows_are_outputs=True, row_chunks=_chunks(tm))


def _proj_gate(h, w, name):
    t = h.shape[0]
    tm, tn = _row_tile(t, 1024), 512

    def epi(acc, rows, x_ref, ex, outs, scr):
        outs[0][rows, :] = _silu(acc).astype(BF16)

    return _proj_call(h, w, COL_GATE, 2 * WIDTH, tm, tn, [], [], [jax.ShapeDtypeStruct((t, 2 * WIDTH), BF16)],
                      [pl.BlockSpec((tm, tn), lambda i, j: (i, j))], epi, name, w_rows_are_outputs=True,
                      row_chunks=_chunks(tm))[0]


def _proj_cq(h, w, g, name):
    t = h.shape[0]
    tm, tn = _row_tile(t, 1024), Q_LORA

    def epi(acc, rows, x_ref, ex, outs, scr):
        outs[0][rows, :] = (_rms(acc) * ex[0][...]).astype(BF16)

    return _proj_call(h, w, COL_CQ, Q_LORA, tm, tn, [g], [_const_spec((1, Q_LORA))],
                      [jax.ShapeDtypeStruct((t, Q_LORA), BF16)], [pl.BlockSpec((tm, tn), lambda i, j: (i, 0))],
                      epi, name, w_rows_are_outputs=True, row_chunks=_chunks(tm))[0]


def _proj_ckv(h, w, g, layer, prev, depth, name):
    t = h.shape[0]
    tm, tn = _row_tile(t, 1024), KV_LORA

    def epi(acc, rows, x_ref, ex, outs, scr):
        v = _rms(acc) * ex[0][...]
        outs[0][rows, :] = v
        outs[1][rows, :] = v.astype(BF16)

    shapes = [jax.ShapeDtypeStruct((depth, t, KV_LORA), F32), jax.ShapeDtypeStruct((t, KV_LORA), BF16)]
    specs = [pl.BlockSpec((None, tm, tn), lambda i, j: (layer, i, 0)), pl.BlockSpec((tm, tn), lambda i, j: (i, 0))]
    alias = [] if prev is None else [(prev, 0)]
    return _proj_call(h, w, COL_CKV, KV_LORA, tm, tn, [g], [_const_spec((1, KV_LORA))], shapes, specs, epi, name,
                      alias_inputs=alias, w_rows_are_outputs=True, row_chunks=_chunks(tm))


def _swap_halves(x):
    lane = lax.broadcasted_iota(jnp.int32, x.shape, 1)
    return jnp.where((lane % ROPE_DIM) < ROPE_DIM // 2, pltpu.roll(x, LANES - ROPE_DIM // 2, 1),
                     pltpu.roll(x, ROPE_DIM // 2, 1))


def _proj_small(h, w, g_kp, b_f_row, cos, sin, layer, prev_kpe, prev_logf, depth, nb, sl, prompt, name):
    t = h.shape[0]
    tm = _row_tile(min(t, sl), 1024) if prompt else t
    nsb = sl // tm if prompt else 1
    seqs_per_blk = 1 if prompt else tm // sl

    def epi(acc, rows, x_ref, ex, outs, scr):
        g_ref, bfr_ref, cos_ref, sin_ref = ex
        kpe_ref, kpe2_ref, logf_ref, cum_ref = outs
        i = pl.program_id(0)
        kp = _rms(acc[:, :LANES], ROPE_DIM) * g_ref[...]
        kp = kp * cos_ref[...] + _swap_halves(kp) * sin_ref[...]
        kpe_ref[...] = kp[:, :ROPE_DIM]
        kpe2_ref[...] = (kp + pltpu.roll(kp, ROPE_DIM, 1)).astype(BF16)
        lf = _log_sigmoid(acc[:, LANES:2 * LANES] + bfr_ref[...])
        logf_ref[...] = lf[:, :HEADS]
        if prompt:
            carry_ref, = scr
            lane = lax.broadcasted_iota(jnp.int32, lf.shape, 1)
            row = lax.broadcasted_iota(jnp.int32, lf.shape, 0)
            c = _scan(jnp.where(lane < HEADS, lf, 0.0), 0, tm, row)
            if nsb > 1:
                @pl.when(i % nsb == 0)
                def _():
                    carry_ref[...] = jnp.zeros_like(carry_ref)
                c = c + carry_ref[0:1, :]
                carry_ref[...] = jnp.broadcast_to(c[tm - 1:tm, :], carry_ref.shape)
            neg = c * (-LOG2E)
            hi = neg.astype(BF16).astype(F32)
            mid = (neg - hi).astype(BF16).astype(F32)
            lo = neg - hi - mid
            cum_ref[...] = (hi + pltpu.roll(mid, HEADS, 1) + pltpu.roll(lo, 2 * HEADS, 1)).astype(BF16)
        else:
            c = lf.T[:HEADS, :]
            pos = lax.broadcasted_iota(jnp.int32, c.shape, 1) % sl
            c = _scan(c, 1, sl, pos) * (-LOG2E)
            cum_ref[...] = jnp.zeros(cum_ref.shape, F32)
            for s in range(seqs_per_blk):
                cum_ref[s, :, 0:sl] = c[:, s * sl:(s + 1) * sl]

    if prompt:
        cum_shape = jax.ShapeDtypeStruct((t, LANES), BF16)
        cum_spec = pl.BlockSpec((tm, LANES), lambda i, j: (i, 0))
        scratch = [pltpu.VMEM((SUBLANES, LANES), F32)]
    else:
        assert sl <= LANES
        cum_shape = jax.ShapeDtypeStruct((nb, HEADS, LANES), F32)
        cum_spec = pl.BlockSpec((seqs_per_blk, HEADS, LANES), lambda i, j: (i, 0, 0))
        scratch = []
    shapes = [jax.ShapeDtypeStruct((depth, t, ROPE_DIM), F32), jax.ShapeDtypeStruct((t, LANES), BF16),
              jax.ShapeDtypeStruct((depth, t, HEADS), F32), cum_shape]
    specs = [pl.BlockSpec((None, tm, ROPE_DIM), lambda i, j: (layer, i, 0)),
             pl.BlockSpec((tm, LANES), lambda i, j: (i, 0)),
             pl.BlockSpec((None, tm, HEADS), lambda i, j: (layer, i, 0)), cum_spec]
    extras = [g_kp, b_f_row, cos, sin]
    especs = [_const_spec((1, LANES)), _const_spec((1, LANES)),
              pl.BlockSpec((tm, LANES), lambda i, j: (i, 0)), pl.BlockSpec((tm, LANES), lambda i, j: (i, 0))]
    alias = []
    if prev_kpe is not None:
        alias = [(prev_kpe, 0), (prev_logf, 2)]
    return _proj_call(h, w, COL_SMALL, N_SMALL, tm, N_SMALL, extras, especs, shapes, specs, epi, name,
                      alias_inputs=alias, scratch_shapes=scratch, w_rows_are_outputs=True)


def _proj_q_pe(cq, w, g2, cos, sin, nb, sl, head_major, name):
    t = cq.shape[0]
    tm, tn = _row_tile(min(t, sl) if head_major else t, 512), HEADS * ROPE_DIM
    shape, spec = _head_out(nb, sl, tm, HEADS, head_major)

    def epi(acc, rows, x_ref, ex, outs, scr):
        g_ref, cos_ref, sin_ref = ex
        lane = lax.broadcasted_iota(jnp.int32, (acc.shape[0], LANES), 1)
        lo = lane < ROPE_DIM
        cos_t, sin_t = cos_ref[rows, :], sin_ref[rows, :]
        for p in range(HEADS // 2):
            x = acc[:, p * LANES:(p + 1) * LANES]
            x2 = x * x
            ms_lo = jnp.sum(jnp.where(lo, x2, 0.0), axis=-1, keepdims=True)
            ms_hi = jnp.sum(jnp.where(lo, 0.0, x2), axis=-1, keepdims=True)
            ms = jnp.where(lo, ms_lo, ms_hi) * (1.0 / ROPE_DIM)
            y = x * lax.rsqrt(ms + EPS) * g_ref[...]
            y = (y * cos_t + _swap_halves(y) * sin_t) * MLA_SCALE
            _store_head(outs[0], 2 * p, rows, jnp.where(lo, y, 0.0).astype(BF16), head_major)
            _store_head(outs[0], 2 * p + 1, rows, jnp.where(lo, 0.0, y).astype(BF16), head_major)

    especs = [_const_spec((1, LANES)), pl.BlockSpec((tm, LANES), lambda i, j: (i, 0)),
              pl.BlockSpec((tm, LANES), lambda i, j: (i, 0))]
    return _proj_call(cq, w, WIDTH, tn, tm, tn, [g2, cos, sin], especs, [shape], [spec], epi, name,
                      row_chunks=_chunks(tm))[0]


def _proj_kv_up(ckv, w, layer, g, nb, sl, name):
    t = ckv.shape[0]
    tm, tn = _row_tile(min(t, sl), 1024), 512
    nh_t = tn // (2 * HEAD_DIM)
    nsb = sl // tm
    shape = jax.ShapeDtypeStruct((nb, HEADS, sl, HEAD_DIM), BF16)
    spec = pl.BlockSpec((None, nh_t, tm, HEAD_DIM), lambda i, j: (i // nsb, j, i % nsb, 0))

    def epi(acc, rows, x_ref, ex, outs, scr):
        for hh in range(nh_t):
            base = hh * 2 * HEAD_DIM
            outs[0][hh, rows, :] = (_rms(acc[:, base:base + HEAD_DIM]) * ex[0][...]).astype(BF16)
            outs[1][hh, rows, :] = acc[:, base + HEAD_DIM:base + 2 * HEAD_DIM].astype(BF16)

    return _proj_call(ckv, w, 0, 2 * WIDTH, tm, tn, [g], [_const_spec((1, HEAD_DIM))], [shape, shape],
                      [spec, spec], epi, name, layer=layer, row_chunks=_chunks(tm))


def _flash_call(q_parts, k_head, k_shared, v, gate, gate_col0, nb, sl, fox, name):
    tq = _row_tile(sl, 512)
    nqb = sl // tq
    t = nb * sl
    n_q = len(q_parts)

    def body(*refs):
        q_refs = refs[:n_q]
        kh_ref, ks_ref, v_ref, g_ref, o_ref = refs[n_q:]
        h = pl.program_id(1)
        key = lax.broadcasted_iota(jnp.int32, (tq, tq), 0)
        qry = lax.broadcasted_iota(jnp.int32, (tq, tq), 1)
        allowed = (key <= qry) if fox else (key // CHUNK <= qry // CHUNK)
        if fox:
            lane = lax.broadcasted_iota(jnp.int32, (tq, LANES), 1)
            selector = jnp.where((lane % HEADS == h) & (lane < 3 * HEADS), 1.0, 0.0).astype(BF16)

        def q_block(qi):
            rows = slice(qi * tq, (qi + 1) * tq)
            parts = [r[rows, :] for r in q_refs] + ([selector] if fox else [])
            return jnp.concatenate(parts, axis=-1)

        def k_block(kb):
            rows = slice(kb * tq, (kb + 1) * tq)
            return jnp.concatenate([kh_ref[rows, :], ks_ref[rows, :]], axis=-1)

        def scores(rnd):
            out = {}
            for qi in range(rnd, nqb):
                s = lax.dot_general(k_block(qi - rnd), q_block(qi), (((1,), (1,)), ((), ())),
                                    preferred_element_type=F32)
                out[qi] = jnp.where(allowed, s, NEG_INF) if rnd == 0 else s
            return out

        state = {qi: (jnp.full((1, tq), NEG_INF, F32), jnp.zeros((1, tq), F32), jnp.zeros((HEAD_DIM, tq), F32))
                 for qi in range(nqb)}
        s_cur = scores(0)
        for rnd in range(nqb):
            s_next = scores(rnd + 1) if rnd + 1 < nqb else {}
            for qi in range(rnd, nqb):
                m, l, acc = state[qi]
                s = s_cur[qi]
                m_new = jnp.maximum(m, jnp.max(s, axis=0, keepdims=True))
                alpha = jnp.exp2(m - m_new)
                p = jnp.exp2(s - m_new)
                l = alpha * l + jnp.sum(p, axis=0, keepdims=True)
                kb = qi - rnd
                pv = lax.dot_general(v_ref[kb * tq:(kb + 1) * tq, :], p.astype(BF16), (((0,), (0,)), ((), ())),
                                     preferred_element_type=F32)
                state[qi] = (m_new, l, alpha * acc + pv)
            s_cur = s_next
        for qi in range(nqb):
            m, l, acc = state[qi]
            rows = slice(qi * tq, (qi + 1) * tq)
            o_ref[rows, :] = ((acc / l).T * g_ref[rows, :].astype(F32)).astype(BF16)

    head_spec = pl.BlockSpec((None, None, sl, HEAD_DIM), lambda b, h: (b, h, 0, 0))
    in_specs = [head_spec for _ in q_parts] + [
        head_spec, pl.BlockSpec((None, sl, LANES), lambda b, h: (b, 0, 0)), head_spec,
        pl.BlockSpec((None, sl, HEAD_DIM), lambda b, h: (b, 0, gate_col0 + h))]
    return pl.pallas_call(
        body,
        out_shape=jax.ShapeDtypeStruct((nb, sl, WIDTH), BF16),
        grid=(nb, HEADS),
        in_specs=in_specs,
        out_specs=pl.BlockSpec((None, sl, HEAD_DIM), lambda b, h: (b, 0, h)),
        compiler_params=_cparams("parallel", "parallel"),
        name=name,
    )(*q_parts, k_head, k_shared, v, gate.reshape(nb, sl, 2 * WIDTH)).reshape(t, WIDTH)


def _pad_rows(x, rows):
    return jnp.concatenate([x, jnp.zeros((rows - x.shape[0], x.shape[1]), x.dtype)], axis=0)


def _fox_decode_call(q, cache_k, cache_v, new_k, new_v, layer, logf_t, cum_new, gate, nb, sd, past, name):
    tk = _row_tile(past, 512)
    nkc = past // tk
    ts = nb * sd
    rows = HEADS * sd

    def body(q_ref, ck_ref, cv_ref, nk_ref, nv_ref, lf_ref, cn_ref, g_ref, o_ref, cum_ref, m_ref, l_ref, acc_ref):
        kc = pl.program_id(1)

        @pl.when(kc == 0)
        def _():
            c = lf_ref[...]
            cum_ref[...] = _scan(c, 1, past, lax.broadcasted_iota(jnp.int32, c.shape, 1)) * (-LOG2E)
            m_ref[...] = jnp.full(m_ref.shape, NEG_INF, F32)
            l_ref[...] = jnp.zeros(l_ref.shape, F32)
            acc_ref[...] = jnp.zeros(acc_ref.shape, F32)

        def attend(keys, values, bias, mask):
            s = []
            for hh in range(HEADS):
                qh = q_ref[:, hh * HEAD_DIM:(hh + 1) * HEAD_DIM]
                sh = lax.dot_general(qh, keys(hh), (((1,), (1,)), ((), ())), preferred_element_type=F32)
                sh = sh + bias(hh)
                s.append(sh if mask is None else jnp.where(mask, sh, NEG_INF))
            s = jnp.concatenate(s, axis=0)
            m_old = m_ref[...]
            m_new = jnp.maximum(m_old, jnp.max(s, axis=-1, keepdims=True))
            alpha = jnp.exp2(m_old - m_new)
            p = jnp.exp2(s - m_new)
            l_ref[...] = alpha * l_ref[...] + jnp.sum(p, axis=-1, keepdims=True)
            m_ref[...] = m_new
            pb = p.astype(BF16)
            pv = [jnp.dot(pb[hh * sd:(hh + 1) * sd, :], values(hh), preferred_element_type=F32)
                  for hh in range(HEADS)]
            acc_ref[...] = alpha * acc_ref[...] + jnp.concatenate(pv, axis=0)

        def by_head(ref, n):
            return jnp.swapaxes(ref[...].astype(BF16).reshape(n, HEADS, HEAD_DIM), 0, 1)

        start = pl.multiple_of(kc * tk, tk)
        kk, vv = by_head(ck_ref, tk), by_head(cv_ref, tk)
        attend(lambda hh: kk[hh], lambda hh: vv[hh], lambda hh: cum_ref[hh:hh + 1, pl.ds(start, tk)], None)

        @pl.when(kc == nkc - 1)
        def _():
            row = lax.broadcasted_iota(jnp.int32, (sd, NEW_PAD), 0)
            col = lax.broadcasted_iota(jnp.int32, (sd, NEW_PAD), 1)
            nk, nv = by_head(nk_ref, sd), by_head(nv_ref, sd)
            attend(lambda hh: _pad_rows(nk[hh], NEW_PAD), lambda hh: _pad_rows(nv[hh], NEW_PAD),
                   lambda hh: cum_ref[hh:hh + 1, past - 1:past] + cn_ref[hh:hh + 1, :], col <= row)
            out = acc_ref[...] / l_ref[...]
            for hh in range(HEADS):
                gh = g_ref[:, hh * HEAD_DIM:(hh + 1) * HEAD_DIM].astype(F32)
                o_ref[:, hh * HEAD_DIM:(hh + 1) * HEAD_DIM] = (out[hh * sd:(hh + 1) * sd, :] * gh).astype(BF16)

    cache_spec = pl.BlockSpec((tk * HEADS, HEAD_DIM), lambda b, kc: ((layer * nb + b) * nkc + kc, 0))
    new_spec = pl.BlockSpec((None, sd * HEADS, HEAD_DIM), lambda b, kc: (layer, b, 0))
    return pl.pallas_call(
        body,
        out_shape=jax.ShapeDtypeStruct((ts, WIDTH), BF16),
        grid=(nb, nkc),
        in_specs=[
            pl.BlockSpec((sd, WIDTH), lambda b, kc: (b, 0)),
            cache_spec, cache_spec, new_spec, new_spec,
            pl.BlockSpec((None, None, HEADS, past), lambda b, kc: (layer, b, 0, 0)),
            pl.BlockSpec((None, HEADS, NEW_PAD), lambda b, kc: (b, 0, 0)),
            pl.BlockSpec((sd, WIDTH), lambda b, kc: (b, 0)),
        ],
        out_specs=pl.BlockSpec((sd, WIDTH), lambda b, kc: (b, 0)),
        scratch_shapes=[pltpu.VMEM((HEADS, past), F32), pltpu.VMEM((rows, 1), F32),
                        pltpu.VMEM((rows, 1), F32), pltpu.VMEM((rows, HEAD_DIM), F32)],
        compiler_params=_cparams("parallel", "arbitrary"),
        name=name,
    )(q, cache_k, cache_v, new_k, new_v, logf_t, cum_new, gate)


def _mla_decode_call(q_nope, q_pe, cache_ckv, cache_kpe_t, new_ckv, new_kpe2, layer, w_kvb, g_kn, gate,
                     nb, sd, past, name):
    ts = nb * sd
    rows = past + NEW_PAD
    hq = HEADS * sd
    per_head = 2 * HEAD_DIM

    def body(qn_ref, qp_ref, cc_ref, ck_ref, nc_ref, nk_ref, w_ref, g_ref, gate_ref, o_ref,
             lat_ref, spe_ref, p_ref, linv_ref, kk_ref):
        hp = pl.program_id(1)

        @pl.when(hp == 0)
        def _():
            lat_ref[0:past, :] = cc_ref[...].astype(BF16)
            lat_ref[past:rows, :] = _pad_rows(nc_ref[...], NEW_PAD)
            q_all = jnp.concatenate([qp_ref[:, h * LANES:(h + 1) * LANES] for h in range(HEADS)], axis=0)
            kp = ck_ref[...].astype(BF16)
            spe_ref[:, 0:past] = jnp.dot(q_all, jnp.concatenate([kp, kp], axis=0), preferred_element_type=F32)
            spe_ref[:, past:rows] = lax.dot_general(q_all, _pad_rows(nk_ref[...], NEW_PAD),
                                                    (((1,), (1,)), ((), ())), preferred_element_type=F32)

        def up_project(pair):
            col = pl.multiple_of(pair * 2 * per_head, 2 * per_head)
            wk = jnp.concatenate([w_ref[:, pl.ds(col, HEAD_DIM)], w_ref[:, pl.ds(col + per_head, HEAD_DIM)]],
                                 axis=1).astype(BF16)
            return jnp.dot(lat_ref[...], wk, preferred_element_type=F32)

        @pl.when(hp == 0)
        def _():
            kk_ref[...] = up_project(0)

        kk_next = up_project(jnp.minimum(hp + 1, HEADS // 2 - 1))
        kk = kk_ref[...]
        q_pos = past + lax.broadcasted_iota(jnp.int32, (sd, rows), 0)
        k_pos = lax.broadcasted_iota(jnp.int32, (sd, rows), 1)
        allowed = (k_pos // CHUNK <= q_pos // CHUNK) & (k_pos < past + sd)
        for g in range(2):
            hs = slice(g * HEAD_DIM, (g + 1) * HEAD_DIM)
            kn = (_rms(kk[:, hs]) * g_ref[...]).astype(BF16)
            r0 = pl.multiple_of((2 * hp + g) * sd, sd)
            s = lax.dot_general(qn_ref[:, hs], kn, (((1,), (1,)), ((), ())), preferred_element_type=F32)
            s = jnp.where(allowed, s + spe_ref[pl.ds(r0, sd), :], NEG_INF)
            p = jnp.exp2(s - jnp.max(s, axis=-1, keepdims=True))
            linv_ref[pl.ds(r0, sd), :] = 1.0 / jnp.sum(p, axis=-1, keepdims=True)
            p_ref[pl.ds(r0, sd), :] = p.astype(BF16)
        kk_ref[...] = kk_next

        @pl.when(hp == HEADS // 2 - 1)
        def _():
            ctx = jnp.dot(p_ref[...], lat_ref[...], preferred_element_type=F32)
            for h in range(HEADS):
                rs = slice(h * sd, (h + 1) * sd)
                wv = w_ref[:, h * per_head + HEAD_DIM:(h + 1) * per_head].astype(BF16)
                out = jnp.dot(ctx[rs, :].astype(BF16), wv, preferred_element_type=F32) * linv_ref[rs, :]
                hs = slice(h * HEAD_DIM, (h + 1) * HEAD_DIM)
                o_ref[:, hs] = (out * gate_ref[:, hs].astype(F32)).astype(BF16)

    return pl.pallas_call(
        body,
        out_shape=jax.ShapeDtypeStruct((ts, WIDTH), BF16),
        grid=(nb, HEADS // 2),
        in_specs=[
            pl.BlockSpec((sd, 2 * HEAD_DIM), lambda b, h: (b, h)),
            pl.BlockSpec((sd, WIDTH), lambda b, h: (b, 0)),
            pl.BlockSpec((None, past, KV_LORA), lambda b, h: (layer * nb + b, 0, 0)),
            pl.BlockSpec((None, ROPE_DIM, past), lambda b, h: (layer * nb + b, 0, 0)),
            pl.BlockSpec((sd, KV_LORA), lambda b, h: (b, 0)),
            pl.BlockSpec((sd, LANES), lambda b, h: (b, 0)),
            pl.BlockSpec((None, KV_LORA, HEADS * per_head), lambda b, h: (layer, 0, 0)),
            pl.BlockSpec((1, HEAD_DIM), lambda b, h: (0, 0)),
            pl.BlockSpec((sd, WIDTH), lambda b, h: (b, 1)),
        ],
        out_specs=pl.BlockSpec((sd, WIDTH), lambda b, h: (b, 0)),
        scratch_shapes=[pltpu.VMEM((rows, KV_LORA), BF16), pltpu.VMEM((hq, rows), F32),
                        pltpu.VMEM((hq, rows), BF16), pltpu.VMEM((hq, 1), F32),
                        pltpu.VMEM((rows, 2 * HEAD_DIM), F32)],
        compiler_params=_cparams("parallel", "arbitrary"),
        name=name,
    )(q_nope, q_pe, cache_ckv, cache_kpe_t, new_ckv, new_kpe2, w_kvb, g_kn, gate)


def _out_proj_call(o_a, o_b, w_out, layer, x, name):
    t, d = x.shape
    tm, tn = _row_tile(t, 1024), 512

    def body(a_ref, b_ref, wa_ref, wb_ref, x_ref, y_ref):
        acc = jnp.dot(a_ref[...], wa_ref[...].astype(BF16), preferred_element_type=F32)
        acc = acc + jnp.dot(b_ref[...], wb_ref[...].astype(BF16), preferred_element_type=F32)
        y_ref[...] = x_ref[...] + acc

    return pl.pallas_call(
        body,
        out_shape=jax.ShapeDtypeStruct((t, d), F32),
        grid=(t // tm, d // tn),
        in_specs=[
            pl.BlockSpec((tm, WIDTH), lambda i, j: (i, 0)),
            pl.BlockSpec((tm, WIDTH), lambda i, j: (i, 0)),
            pl.BlockSpec((None, WIDTH, tn), lambda i, j: (layer, 0, j)),
            pl.BlockSpec((None, WIDTH, tn), lambda i, j: (layer, 1, j)),
            pl.BlockSpec((tm, tn), lambda i, j: (i, j)),
        ],
        out_specs=pl.BlockSpec((tm, tn), lambda i, j: (i, j)),
        compiler_params=_cparams("parallel", "parallel"),
        name=name,
    )(o_a, o_b, w_out, w_out, x)


def _rope_tables(pos):
    half = ROPE_DIM // 2
    inv_freq = 1.0 / (ROPE_THETA ** (jnp.arange(half, dtype=F32) / half))
    ang = pos.astype(F32)[:, None] * inv_freq[None, :]
    cos, sin = jnp.cos(ang), jnp.sin(ang)
    cos64 = jnp.concatenate([cos, cos], axis=-1)
    sin64 = jnp.concatenate([-sin, sin], axis=-1)
    return jnp.concatenate([cos64, cos64], axis=-1), jnp.concatenate([sin64, sin64], axis=-1)


def _prep_w_in_call(w_in_t, layer, name):
    _, n_in, d = w_in_t.shape
    tc = _row_tile(d, 256)
    src = {}
    o = 0
    for seg, size in (("q", WIDTH), ("k", WIDTH), ("v", WIDTH), ("f", HEADS), ("za", WIDTH),
                      ("cq", Q_LORA), ("ckv", KV_LORA), ("kpe", ROPE_DIM), ("zb", WIDTH)):
        src[seg] = (o, size)
        o += size
    assert o == n_in
    dst = (("q", COL_Q), ("k", COL_K), ("v", COL_V), ("za", COL_GATE), ("zb", COL_GATE + WIDTH), ("cq", COL_CQ),
           ("ckv", COL_CKV), ("kpe", COL_SMALL), ("f", COL_SMALL + LANES))

    def body(x_ref, o_ref):
        for seg, d0 in dst:
            s0, size = src[seg]
            o_ref[d0:d0 + size, :] = x_ref[s0:s0 + size, :].astype(BF16)
        pad0 = COL_SMALL + ROPE_DIM
        o_ref[pad0:COL_SMALL + LANES, :] = jnp.zeros((LANES - ROPE_DIM, tc), BF16)
        pad1 = COL_SMALL + LANES + HEADS
        o_ref[pad1:COL_SMALL + N_SMALL, :] = jnp.zeros((LANES - HEADS, tc), BF16)

    return pl.pallas_call(
        body,
        out_shape=jax.ShapeDtypeStruct((COL_SMALL + N_SMALL, d), BF16),
        grid=(d // tc,),
        in_specs=[pl.BlockSpec((None, n_in, tc), lambda i: (layer, 0, i))],
        out_specs=pl.BlockSpec((COL_SMALL + N_SMALL, tc), lambda i: (0, i)),
        compiler_params=_cparams("parallel"),
        name=name,
    )(w_in_t)


def _prep_w_qb_call(w_qb, layer, name):
    _, k, n = w_qb.shape
    per_head = NOPE_DIM + ROPE_DIM
    tr = _row_tile(k, 256)

    def body(x_ref, o_ref):
        for hh in range(HEADS):
            o_ref[:, hh * NOPE_DIM:(hh + 1) * NOPE_DIM] = x_ref[:, hh * per_head:hh * per_head + NOPE_DIM].astype(BF16)
        for p in range(HEADS // 2):
            pe = [x_ref[:, hh * per_head + NOPE_DIM:(hh + 1) * per_head] for hh in (2 * p, 2 * p + 1)]
            o_ref[:, WIDTH + p * LANES:WIDTH + (p + 1) * LANES] = jnp.concatenate(pe, axis=1).astype(BF16)

    return pl.pallas_call(
        body,
        out_shape=jax.ShapeDtypeStruct((k, n), BF16),
        grid=(k // tr,),
        in_specs=[pl.BlockSpec((None, tr, n), lambda i: (layer, i, 0))],
        out_specs=pl.BlockSpec((tr, n), lambda i: (i, 0)),
        compiler_params=_cparams("parallel"),
        name=name,
    )(w_qb)


def _layer_params(l, g_norm, b_f, g_q_fox, g_k_fox, g_cq, g_qn, g_qp, g_ckv, g_kp, g_kn):
    row = lambda a: a.reshape(1, -1).astype(F32)
    pad = lambda a: jnp.concatenate([row(a), jnp.zeros((1, LANES - a.shape[-1]), F32)], axis=1)
    return dict(
        g_norm=row(g_norm[l]), b_f_row=pad(b_f[l]), g_q=row(g_q_fox[l]), g_k=row(g_k_fox[l]), g_cq=row(g_cq[l]),
        g_qn=row(g_qn[l]), g_qp2=jnp.concatenate([row(g_qp[l]), row(g_qp[l])], axis=1),
        g_ckv=row(g_ckv[l]), g_kp=pad(g_kp[l]), g_kn=row(g_kn[l]),
    )


def _input_stage(x, w, w_qb, p, cos, sin, layer, prev, depth, nb, sl, prompt, tag):
    h = _rmsnorm_call(x, p["g_norm"], f"norm_{tag}")
    prev = prev or {}
    r = {}
    r["q"] = _proj_heads(h, w, COL_Q, p["g_q"], FOX_SCALE, nb, sl, prompt, f"proj_q_{tag}",
                         w_rows_are_outputs=True)
    k_out = _proj_kv_fox(h, w, COL_K, p["g_k"], layer, prev.get("k"), depth, nb, sl, prompt, f"proj_k_{tag}")
    v_out = _proj_kv_fox(h, w, COL_V, None, layer, prev.get("v"), depth, nb, sl, prompt, f"proj_v_{tag}")
    r["k"], r["v"] = k_out[0], v_out[0]
    if prompt:
        r["k_bf"], r["v_bf"] = k_out[1], v_out[1]
    r["gate"] = _proj_gate(h, w, f"proj_gate_{tag}")
    cq = _proj_cq(h, w, p["g_cq"], f"proj_cq_{tag}")
    r["ckv"], r["ckv_bf"] = _proj_ckv(h, w, p["g_ckv"], layer, prev.get("ckv"), depth, f"proj_ckv_{tag}")
    r["kpe"], r["kpe2"], r["logf"], r["cum"] = _proj_small(
        h, w, p["g_kp"], p["b_f_row"], cos, sin, layer, prev.get("kpe"), prev.get("logf"), depth, nb, sl, prompt,
        f"proj_small_{tag}")
    r["q_nope"] = _proj_heads(cq, w_qb, 0, p["g_qn"], MLA_SCALE, nb, sl, prompt, f"proj_qn_{tag}")
    r["q_pe"] = _proj_q_pe(cq, w_qb, p["g_qp2"], cos, sin, nb, sl, prompt, f"proj_qp_{tag}")
    return r


def kernel(x_prompt, x_sample, cache_fox_k, cache_fox_v, cache_fox_logf, cache_mla_ckv, cache_mla_kpe, g_norm,
           w_in, b_f, g_q_fox, g_k_fox, g_cq, w_qb, g_qn, g_qp, g_ckv, g_kp, w_kvb, g_kn, w_out):
    nb, sl, d = x_prompt.shape
    nbd, sd, _ = x_sample.shape
    depth = w_in.shape[0]
    past = cache_fox_k.shape[2]
    assert sl % CHUNK == 0 and past % CHUNK == 0 and sd <= CHUNK, "chunk-aligned streaming shapes only"

    cos_p, sin_p = _rope_tables(jnp.arange(sl, dtype=jnp.int32))
    cos_p, sin_p = jnp.tile(cos_p, (nb, 1)), jnp.tile(sin_p, (nb, 1))
    cos_s, sin_s = _rope_tables(past + jnp.arange(sd, dtype=jnp.int32))
    cos_s, sin_s = jnp.tile(cos_s, (nbd, 1)), jnp.tile(sin_s, (nbd, 1))

    ck2 = cache_fox_k.reshape(-1, HEAD_DIM)
    cv2 = cache_fox_v.reshape(-1, HEAD_DIM)
    logf_t = jnp.swapaxes(cache_fox_logf, 2, 3)
    w_in_t = jnp.swapaxes(w_in, 1, 2)
    cckv = cache_mla_ckv.reshape(depth * nbd, past, KV_LORA)
    ckpe = jnp.swapaxes(cache_mla_kpe, 2, 3).reshape(depth * nbd, ROPE_DIM, past)

    y_p = x_prompt.reshape(nb * sl, d)
    y_s = x_sample.reshape(nbd * sd, d)
    prev_p, prev_s = None, None
    for l in range(depth):
        p = _layer_params(l, g_norm, b_f, g_q_fox, g_k_fox, g_cq, g_qn, g_qp, g_ckv, g_kp, g_kn)
        w = _prep_w_in_call(w_in_t, l, f"prep_w_in_{l}")
        wq = _prep_w_qb_call(w_qb, l, f"prep_w_qb_{l}")
        rp = _input_stage(y_p, w, wq, p, cos_p, sin_p, l, prev_p, depth, nb, sl, True, f"p{l}")
        kn_p, v_p = _proj_kv_up(rp["ckv_bf"], w_kvb, l, p["g_kn"], nb, sl, f"kv_up_p{l}")
        o_a = _flash_call([rp["q"]], rp["k_bf"], rp["cum"].reshape(nb, sl, LANES), rp["v_bf"], rp["gate"], 0,
                          nb, sl, True, f"fox_attn_p{l}")
        o_b = _flash_call([rp["q_nope"], rp["q_pe"]], kn_p, rp["kpe2"].reshape(nb, sl, LANES), v_p, rp["gate"],
                          HEADS, nb, sl, False, f"mla_attn_p{l}")
        y_p = _out_proj_call(o_a, o_b, w_out, l, y_p, f"out_proj_p{l}")
        prev_p = rp
        rs = _input_stage(y_s, w, wq, p, cos_s, sin_s, l, prev_s, depth, nbd, sd, False, f"s{l}")
        o_a = _fox_decode_call(rs["q"], ck2, cv2, rs["k"], rs["v"], l, logf_t, rs["cum"], rs["gate"], nbd, sd,
                               past, f"fox_attn_s{l}")
        o_b = _mla_decode_call(rs["q_nope"], rs["q_pe"], cckv, ckpe, rs["ckv_bf"], rs["kpe2"], l, w_kvb,
                               p["g_kn"], rs["gate"], nbd, sd, past, f"mla_attn_s{l}")
        y_s = _out_proj_call(o_a, o_b, w_out, l, y_s, f"out_proj_s{l}")
        prev_s = rs

    hk = (HEADS, HEAD_DIM)
    return (y_p.reshape(nb, sl, d), y_s.reshape(nbd, sd, d),
            prev_p["k"].reshape((depth, nb, sl) + hk), prev_p["v"].reshape((depth, nb, sl) + hk),
            prev_p["logf"].reshape(depth, nb, sl, HEADS), prev_p["ckv"].reshape(depth, nb, sl, KV_LORA),
            prev_p["kpe"].reshape(depth, nb, sl, ROPE_DIM),
            prev_s["k"].reshape((depth, nbd, sd) + hk), prev_s["v"].reshape((depth, nbd, sd) + hk),
            prev_s["logf"].reshape(depth, nbd, sd, HEADS), prev_s["ckv"].reshape(depth, nbd, sd, KV_LORA),
            prev_s["kpe"].reshape(depth, nbd, sd, ROPE_DIM))
```

```python
import math

import jax
import jax.numpy as jnp
from jax import lax
from jax.experimental import pallas as pl
from jax.experimental.pallas import tpu as pltpu

F32 = jnp.float32
BF16 = jnp.bfloat16

CHUNK = 64
HEADS = 16
HEAD_DIM = 128
WIDTH = HEADS * HEAD_DIM
NOPE_DIM = 128
ROPE_DIM = 64
Q_LORA = 1024
KV_LORA = 512
ROPE_THETA = 10000.0
EPS = 1e-6
LOG2E = math.log2(math.e)
FOX_SCALE = LOG2E / math.sqrt(HEAD_DIM)
MLA_SCALE = LOG2E / math.sqrt(NOPE_DIM + ROPE_DIM)
LANES = 128
SUBLANES = 8
NEW_PAD = 128
VMEM_LIMIT_BYTES = 52 * 1024 * 1024
NEG_INF = float("-inf")

COL_Q, COL_K, COL_V, COL_GATE = 0, WIDTH, 2 * WIDTH, 3 * WIDTH
COL_CQ = 5 * WIDTH
COL_CKV = COL_CQ + Q_LORA
COL_SMALL = COL_CKV + KV_LORA
N_SMALL = 2 * LANES


def _cparams(*sem):
    return pltpu.CompilerParams(dimension_semantics=sem, vmem_limit_bytes=VMEM_LIMIT_BYTES)


def _row_tile(t, cap):
    if t <= cap:
        return t
    for c in range(cap, 15, -16):
        if t % c == 0:
            return c
    return t


def _rms(x, n=None, axis=-1):
    n = x.shape[axis] if n is None else n
    ms = jnp.sum(x * x, axis=axis, keepdims=True) * (1.0 / n)
    return x * lax.rsqrt(ms + EPS)


def _silu(x):
    return x * (1.0 / (1.0 + jnp.exp(-x)))


def _log_sigmoid(x):
    return jnp.minimum(x, 0.0) - jnp.log1p(jnp.exp(-jnp.abs(x)))


def _scan(x, axis, length, pos):
    sh = 1
    while sh < length:
        x = x + jnp.where(pos >= sh, pltpu.roll(x, sh, axis), 0.0)
        sh *= 2
    return x


def _rmsnorm_call(x, g, name):
    t, d = x.shape
    tm = _row_tile(t, 256)

    def body(x_ref, g_ref, o_ref):
        o_ref[...] = (_rms(x_ref[...]) * g_ref[...]).astype(BF16)

    return pl.pallas_call(
        body,
        out_shape=jax.ShapeDtypeStruct((t, d), BF16),
        grid=(t // tm,),
        in_specs=[pl.BlockSpec((tm, d), lambda i: (i, 0)), pl.BlockSpec((1, d), lambda i: (0, 0))],
        out_specs=pl.BlockSpec((tm, d), lambda i: (i, 0)),
        compiler_params=_cparams("parallel"),
        name=name,
    )(x, g)


def _proj_call(x, w, col0, n, tm, tn, extras, extra_specs, out_shapes, out_specs, epilogue, name,
               alias_inputs=(), scratch_shapes=(), layer=None, w_rows_are_outputs=False, row_chunks=1):
    t, k = x.shape
    assert col0 % tn == 0 and n % tn == 0 and t % tm == 0
    cb0 = col0 // tn
    n_ex, n_al, n_out = len(extras), len(alias_inputs), len(out_shapes)
    contract = (((1,), (1,)), ((), ())) if w_rows_are_outputs else (((1,), (0,)), ((), ()))

    assert tm % (16 * row_chunks) == 0
    nr = tm // row_chunks

    def body(x_ref, w_ref, *refs):
        ex = refs[:n_ex]
        outs = refs[n_ex + n_al:n_ex + n_al + n_out]
        scr = refs[n_ex + n_al + n_out:]
        wv = w_ref[...].astype(BF16)
        for c in range(row_chunks):
            rows = slice(c * nr, (c + 1) * nr)
            acc = lax.dot_general(x_ref[rows, :], wv, contract, preferred_element_type=F32)
            epilogue(acc, rows, x_ref, ex, outs, scr)

    if w_rows_are_outputs:
        assert layer is None
        w_spec = pl.BlockSpec((tn, k), lambda i, j: (cb0 + j, 0))
    elif layer is None:
        w_spec = pl.BlockSpec((k, tn), lambda i, j: (0, cb0 + j))
    else:
        w_spec = pl.BlockSpec((None, k, tn), lambda i, j: (layer, 0, cb0 + j))
    in_specs = [pl.BlockSpec((tm, k), lambda i, j: (i, 0)), w_spec]
    in_specs += list(extra_specs)
    in_specs += [pl.BlockSpec(memory_space=pl.ANY) for _ in alias_inputs]
    aliases = {2 + n_ex + a: out_idx for a, (_, out_idx) in enumerate(alias_inputs)}
    return pl.pallas_call(
        body,
        out_shape=out_shapes,
        grid=(t // tm, n // tn),
        in_specs=in_specs,
        out_specs=out_specs,
        scratch_shapes=list(scratch_shapes),
        input_output_aliases=aliases,
        compiler_params=_cparams("arbitrary", "arbitrary"),
        name=name,
    )(x, w, *extras, *[a for a, _ in alias_inputs])


def _chunks(tm):
    return 2 if tm % 512 == 0 else 1


def _const_spec(shape):
    nd = len(shape)
    return pl.BlockSpec(shape, lambda i, j: (0,) * nd)


def _head_out(nb, sl, tm, nh_t, head_major):
    if head_major:
        nsb = sl // tm
        shape = jax.ShapeDtypeStruct((nb, HEADS, sl, HEAD_DIM), BF16)
        spec = pl.BlockSpec((None, nh_t, tm, HEAD_DIM), lambda i, j: (i // nsb, j, i % nsb, 0))
    else:
        shape = jax.ShapeDtypeStruct((nb * sl, WIDTH), BF16)
        spec = pl.BlockSpec((tm, nh_t * HEAD_DIM), lambda i, j: (i, j))
    return shape, spec


def _store_head(o_ref, hh, rows, val, head_major):
    if head_major:
        o_ref[hh, rows, :] = val
    else:
        o_ref[rows, hh * HEAD_DIM:(hh + 1) * HEAD_DIM] = val


def _proj_heads(h, w, col0, g, scale, nb, sl, head_major, name, w_rows_are_outputs=False):
    t, k = h.shape
    tm, tn = _row_tile(min(t, sl) if head_major else t, 1024), (1024 if k <= 1024 else 512)
    nh_t = tn // HEAD_DIM
    shape, spec = _head_out(nb, sl, tm, nh_t, head_major)

    def epi(acc, rows, x_ref, ex, outs, scr):
        g_ref, = ex
        for hh in range(nh_t):
            v = _rms(acc[:, hh * HEAD_DIM:(hh + 1) * HEAD_DIM]) * g_ref[...] * scale
            _store_head(outs[0], hh, rows, v.astype(BF16), head_major)

    return _proj_call(h, w, col0, WIDTH, tm, tn, [g], [_const_spec((1, HEAD_DIM))], [shape], [spec], epi, name,
                      w_rows_are_outputs=w_rows_are_outputs, row_chunks=_chunks(tm))[0]


def _proj_kv_fox(h, w, col0, g, layer, prev, depth, nb, sl, head_major, name):
    t = h.shape[0]
    tm, tn = _row_tile(min(t, sl) if head_major else t, 1024), 512
    nh_t = tn // HEAD_DIM
    f_shape = jax.ShapeDtypeStruct((depth, t * HEADS, HEAD_DIM), F32)
    f_spec = pl.BlockSpec((None, tm * HEADS, HEAD_DIM), lambda i, j: (layer, i, 0))
    shapes, specs = [f_shape], [f_spec]
    if head_major:
        b_shape, b_spec = _head_out(nb, sl, tm, nh_t, True)
        shapes.append(b_shape)
        specs.append(b_spec)
    extras = [] if g is None else [g]
    especs = [] if g is None else [_const_spec((1, HEAD_DIM))]

    def epi(acc, rows, x_ref, ex, outs, scr):
        j = pl.program_id(1)
        for hh in range(nh_t):
            v = acc[:, hh * HEAD_DIM:(hh + 1) * HEAD_DIM]
            if g is not None:
                v = _rms(v) * ex[0][...]
            nr = rows.stop - rows.start
            outs[0][pl.ds(rows.start * HEADS + j * nh_t + hh, nr, stride=HEADS), :] = v
            if head_major:
                outs[1][hh, rows, :] = v.astype(BF16)

    alias = [] if prev is None else [(prev, 0)]
    return _proj_call(h, w, col0, WIDTH, tm, tn, extras, especs, shapes, specs, epi, name, alias_inputs=alias,
                      w_rows_are_outputs=True, row_chunks=_chunks(tm))


def _proj_gate(h, w, name):
    t = h.shape[0]
    tm, tn = _row_tile(t, 1024), 512

    def epi(acc, rows, x_ref, ex, outs, scr):
        outs[0][rows, :] = _silu(acc).astype(BF16)

    return _proj_call(h, w, COL_GATE, 2 * WIDTH, tm, tn, [], [], [jax.ShapeDtypeStruct((t, 2 * WIDTH), BF16)],
                      [pl.BlockSpec((tm, tn), lambda i, j: (i, j))], epi, name, w_rows_are_outputs=True,
                      row_chunks=_chunks(tm))[0]


def _proj_cq(h, w, g, name):
    t = h.shape[0]
    tm, tn = _row_tile(t, 1024), Q_LORA

    def epi(acc, rows, x_ref, ex, outs, scr):
        outs[0][rows, :] = (_rms(acc) * ex[0][...]).astype(BF16)

    return _proj_call(h, w, COL_CQ, Q_LORA, tm, tn, [g], [_const_spec((1, Q_LORA))],
                      [jax.ShapeDtypeStruct((t, Q_LORA), BF16)], [pl.BlockSpec((tm, tn), lambda i, j: (i, 0))],
                      epi, name, w_rows_are_outputs=True, row_chunks=_chunks(tm))[0]


def _proj_ckv(h, w, g, layer, prev, depth, name):
    t = h.shape[0]
    tm, tn = _row_tile(t, 1024), KV_LORA

    def epi(acc, rows, x_ref, ex, outs, scr):
        v = _rms(acc) * ex[0][...]
        outs[0][rows, :] = v
        outs[1][rows, :] = v.astype(BF16)

    shapes = [jax.ShapeDtypeStruct((depth, t, KV_LORA), F32), jax.ShapeDtypeStruct((t, KV_LORA), BF16)]
    specs = [pl.BlockSpec((None, tm, tn), lambda i, j: (layer, i, 0)), pl.BlockSpec((tm, tn), lambda i, j: (i, 0))]
    alias = [] if prev is None else [(prev, 0)]
    return _proj_call(h, w, COL_CKV, KV_LORA, tm, tn, [g], [_const_spec((1, KV_LORA))], shapes, specs, epi, name,
                      alias_inputs=alias, w_rows_are_outputs=True, row_chunks=_chunks(tm))


def _swap_halves(x):
    lane = lax.broadcasted_iota(jnp.int32, x.shape, 1)
    return jnp.where((lane % ROPE_DIM) < ROPE_DIM // 2, pltpu.roll(x, LANES - ROPE_DIM // 2, 1),
                     pltpu.roll(x, ROPE_DIM // 2, 1))


def _proj_small(h, w, g_kp, b_f_row, cos, sin, layer, prev_kpe, prev_logf, depth, nb, sl, prompt, name):
    t = h.shape[0]
    tm = _row_tile(min(t, sl), 1024) if prompt else t
    nsb = sl // tm if prompt else 1
    seqs_per_blk = 1 if prompt else tm // sl

    def epi(acc, rows, x_ref, ex, outs, scr):
        g_ref, bfr_ref, cos_ref, sin_ref = ex
        kpe_ref, kpe2_ref, logf_ref, cum_ref = outs
        i = pl.program_id(0)
        kp = _rms(acc[:, :LANES], ROPE_DIM) * g_ref[...]
        kp = kp * cos_ref[...] + _swap_halves(kp) * sin_ref[...]
        kpe_ref[...] = kp[:, :ROPE_DIM]
        kpe2_ref[...] = (kp + pltpu.roll(kp, ROPE_DIM, 1)).astype(BF16)
        lf = _log_sigmoid(acc[:, LANES:2 * LANES] + bfr_ref[...])
        logf_ref[...] = lf[:, :HEADS]
        if prompt:
            carry_ref, = scr
            lane = lax.broadcasted_iota(jnp.int32, lf.shape, 1)
            row = lax.broadcasted_iota(jnp.int32, lf.shape, 0)
            c = _scan(jnp.where(lane < HEADS, lf, 0.0), 0, tm, row)
            if nsb > 1:
                @pl.when(i % nsb == 0)
                def _():
                    carry_ref[...] = jnp.zeros_like(carry_ref)
                c = c + carry_ref[0:1, :]
                carry_ref[...] = jnp.broadcast_to(c[tm - 1:tm, :], carry_ref.shape)
            neg = c * (-LOG2E)
            hi = neg.astype(BF16).astype(F32)
            mid = (neg - hi).astype(BF16).astype(F32)
            lo = neg - hi - mid
            cum_ref[...] = (hi + pltpu.roll(mid, HEADS, 1) + pltpu.roll(lo, 2 * HEADS, 1)).astype(BF16)
        else:
            c = lf.T[:HEADS, :]
            pos = lax.broadcasted_iota(jnp.int32, c.shape, 1) % sl
            c = _scan(c, 1, sl, pos) * (-LOG2E)
            cum_ref[...] = jnp.zeros(cum_ref.shape, F32)
            for s in range(seqs_per_blk):
                cum_ref[s, :, 0:sl] = c[:, s * sl:(s + 1) * sl]

    if prompt:
        cum_shape = jax.ShapeDtypeStruct((t, LANES), BF16)
        cum_spec = pl.BlockSpec((tm, LANES), lambda i, j: (i, 0))
        scratch = [pltpu.VMEM((SUBLANES, LANES), F32)]
    else:
        assert sl <= LANES
        cum_shape = jax.ShapeDtypeStruct((nb, HEADS, LANES), F32)
        cum_spec = pl.BlockSpec((seqs_per_blk, HEADS, LANES), lambda i, j: (i, 0, 0))
        scratch = []
    shapes = [jax.ShapeDtypeStruct((depth, t, ROPE_DIM), F32), jax.ShapeDtypeStruct((t, LANES), BF16),
              jax.ShapeDtypeStruct((depth, t, HEADS), F32), cum_shape]
    specs = [pl.BlockSpec((None, tm, ROPE_DIM), lambda i, j: (layer, i, 0)),
             pl.BlockSpec((tm, LANES), lambda i, j: (i, 0)),
             pl.BlockSpec((None, tm, HEADS), lambda i, j: (layer, i, 0)), cum_spec]
    extras = [g_kp, b_f_row, cos, sin]
    especs = [_const_spec((1, LANES)), _const_spec((1, LANES)),
              pl.BlockSpec((tm, LANES), lambda i, j: (i, 0)), pl.BlockSpec((tm, LANES), lambda i, j: (i, 0))]
    alias = []
    if prev_kpe is not None:
        alias = [(prev_kpe, 0), (prev_logf, 2)]
    return _proj_call(h, w, COL_SMALL, N_SMALL, tm, N_SMALL, extras, especs, shapes, specs, epi, name,
                      alias_inputs=alias, scratch_shapes=scratch, w_rows_are_outputs=True)


def _proj_q_pe(cq, w, g2, cos, sin, nb, sl, head_major, name):
    t = cq.shape[0]
    tm, tn = _row_tile(min(t, sl) if head_major else t, 512), HEADS * ROPE_DIM
    shape, spec = _head_out(nb, sl, tm, HEADS, head_major)

    def epi(acc, rows, x_ref, ex, outs, scr):
        g_ref, cos_ref, sin_ref = ex
        lane = lax.broadcasted_iota(jnp.int32, (acc.shape[0], LANES), 1)
        lo = lane < ROPE_DIM
        cos_t, sin_t = cos_ref[rows, :], sin_ref[rows, :]
        for p in range(HEADS // 2):
            x = acc[:, p * LANES:(p + 1) * LANES]
            x2 = x * x
            ms_lo = jnp.sum(jnp.where(lo, x2, 0.0), axis=-1, keepdims=True)
            ms_hi = jnp.sum(jnp.where(lo, 0.0, x2), axis=-1, keepdims=True)
            ms = jnp.where(lo, ms_lo, ms_hi) * (1.0 / ROPE_DIM)
            y = x * lax.rsqrt(ms + EPS) * g_ref[...]
            y = (y * cos_t + _swap_halves(y) * sin_t) * MLA_SCALE
            _store_head(outs[0], 2 * p, rows, jnp.where(lo, y, 0.0).astype(BF16), head_major)
            _store_head(outs[0], 2 * p + 1, rows, jnp.where(lo, 0.0, y).astype(BF16), head_major)

    especs = [_const_spec((1, LANES)), pl.BlockSpec((tm, LANES), lambda i, j: (i, 0)),
              pl.BlockSpec((tm, LANES), lambda i, j: (i, 0))]
    return _proj_call(cq, w, WIDTH, tn, tm, tn, [g2, cos, sin], especs, [shape], [spec], epi, name,
                      row_chunks=_chunks(tm))[0]


def _proj_kv_up(ckv, w, layer, g, nb, sl, name):
    t = ckv.shape[0]
    tm, tn = _row_tile(min(t, sl), 1024), 1024
    nh_t = tn // (2 * HEAD_DIM)
    nsb = sl // tm
    shape = jax.ShapeDtypeStruct((nb, HEADS, sl, HEAD_DIM), BF16)
    spec = pl.BlockSpec((None, nh_t, tm, HEAD_DIM), lambda i, j: (i // nsb, j, i % nsb, 0))

    def epi(acc, rows, x_ref, ex, outs, scr):
        for hh in range(nh_t):
            base = hh * 2 * HEAD_DIM
            outs[0][hh, rows, :] = (_rms(acc[:, base:base + HEAD_DIM]) * ex[0][...]).astype(BF16)
            outs[1][hh, rows, :] = acc[:, base + HEAD_DIM:base + 2 * HEAD_DIM].astype(BF16)

    return _proj_call(ckv, w, 0, 2 * WIDTH, tm, tn, [g], [_const_spec((1, HEAD_DIM))], [shape, shape],
                      [spec, spec], epi, name, layer=layer, row_chunks=_chunks(tm))


FLASH_HEADS_PER_STEP = 2


def _flash_call(q_parts, k_head, k_shared, v, gate, gate_col0, nb, sl, fox, name):
    tq = _row_tile(sl, 256)
    nqb = sl // tq
    t = nb * sl
    n_q = len(q_parts)

    hps = FLASH_HEADS_PER_STEP
    chains = [(hh, qi) for hh in range(hps) for qi in range(nqb)]

    def body(*refs):
        q_refs = refs[:n_q]
        kh_ref, ks_ref, v_ref, g_ref, o_ref = refs[n_q:]
        hp = pl.program_id(1)
        key = lax.broadcasted_iota(jnp.int32, (tq, tq), 0)
        qry = lax.broadcasted_iota(jnp.int32, (tq, tq), 1)
        allowed = (key <= qry) if fox else (key // CHUNK <= qry // CHUNK)
        if fox:
            lane = lax.broadcasted_iota(jnp.int32, (tq, LANES), 1)
            selectors = [jnp.where((lane % HEADS == hp * hps + hh) & (lane < 3 * HEADS), 1.0, 0.0).astype(BF16)
                         for hh in range(hps)]

        def q_block(hh, qi):
            rows = slice(qi * tq, (qi + 1) * tq)
            parts = [r[hh, rows, :] for r in q_refs] + ([selectors[hh]] if fox else [])
            return jnp.concatenate(parts, axis=-1)

        def k_block(hh, kb):
            rows = slice(kb * tq, (kb + 1) * tq)
            return jnp.concatenate([kh_ref[hh, rows, :], ks_ref[rows, :]], axis=-1)

        def score(rnd, hh, qi):
            s = lax.dot_general(k_block(hh, qi - rnd), q_block(hh, qi), (((1,), (1,)), ((), ())),
                                preferred_element_type=F32)
            return jnp.where(allowed, s, NEG_INF) if rnd == 0 else s

        state = {c: (jnp.full((1, tq), NEG_INF, F32), jnp.zeros((1, tq), F32), jnp.zeros((HEAD_DIM, tq), F32))
                 for c in chains}
        s_cur = {c: score(0, *c) for c in chains}
        for rnd in range(nqb):
            s_next = {}
            for hh, qi in chains:
                if qi < rnd:
                    continue
                if qi >= rnd + 1 and rnd + 1 < nqb:
                    s_next[hh, qi] = score(rnd + 1, hh, qi)
                m, l, acc = state[hh, qi]
                s = s_cur[hh, qi]
                m_new = jnp.maximum(m, jnp.max(s, axis=0, keepdims=True))
                alpha = jnp.exp2(m - m_new)
                p = jnp.exp2(s - m_new)
                l = alpha * l + jnp.sum(p, axis=0, keepdims=True)
                kb = qi - rnd
                pv = lax.dot_general(v_ref[hh, kb * tq:(kb + 1) * tq, :], p.astype(BF16),
                                     (((0,), (0,)), ((), ())), preferred_element_type=F32)
                state[hh, qi] = (m_new, l, alpha * acc + pv)
            s_cur = s_next
        for hh, qi in chains:
            m, l, acc = state[hh, qi]
            rows = slice(qi * tq, (qi + 1) * tq)
            cols = slice(hh * HEAD_DIM, (hh + 1) * HEAD_DIM)
            o_ref[rows, cols] = ((acc / l).T * g_ref[rows, cols].astype(F32)).astype(BF16)

    head_spec = pl.BlockSpec((None, hps, sl, HEAD_DIM), lambda b, h: (b, h, 0, 0))
    gate_cb0 = gate_col0 // hps
    in_specs = [head_spec for _ in q_parts] + [
        head_spec, pl.BlockSpec((None, sl, LANES), lambda b, h: (b, 0, 0)), head_spec,
        pl.BlockSpec((None, sl, hps * HEAD_DIM), lambda b, h: (b, 0, gate_cb0 + h))]
    return pl.pallas_call(
        body,
        out_shape=jax.ShapeDtypeStruct((nb, sl, WIDTH), BF16),
        grid=(nb, HEADS // hps),
        in_specs=in_specs,
        out_specs=pl.BlockSpec((None, sl, hps * HEAD_DIM), lambda b, h: (b, 0, h)),
        compiler_params=_cparams("parallel", "parallel"),
        name=name,
    )(*q_parts, k_head, k_shared, v, gate.reshape(nb, sl, 2 * WIDTH)).reshape(t, WIDTH)


def _pad_rows(x, rows):
    return jnp.concatenate([x, jnp.zeros((rows - x.shape[0], x.shape[1]), x.dtype)], axis=0)


def _fox_decode_call(q, cache_k, cache_v, new_k, new_v, layer, logf_t, cum_new, gate, nb, sd, past, name):
    tk = _row_tile(past, 512)
    nkc = past // tk
    ts = nb * sd
    rows = HEADS * sd

    def body(q_ref, ck_ref, cv_ref, nk_ref, nv_ref, lf_ref, cn_ref, g_ref, o_ref, cum_ref, m_ref, l_ref, acc_ref):
        kc = pl.program_id(1)

        @pl.when(kc == 0)
        def _():
            c = lf_ref[...]
            cum_ref[...] = _scan(c, 1, past, lax.broadcasted_iota(jnp.int32, c.shape, 1)) * (-LOG2E)
            m_ref[...] = jnp.full(m_ref.shape, NEG_INF, F32)
            l_ref[...] = jnp.zeros(l_ref.shape, F32)
            acc_ref[...] = jnp.zeros(acc_ref.shape, F32)

        def attend(keys, values, bias, mask):
            s = []
            for hh in range(HEADS):
                qh = q_ref[:, hh * HEAD_DIM:(hh + 1) * HEAD_DIM]
                sh = lax.dot_general(qh, keys(hh), (((1,), (1,)), ((), ())), preferred_element_type=F32)
                sh = sh + bias(hh)
                s.append(sh if mask is None else jnp.where(mask, sh, NEG_INF))
            s = jnp.concatenate(s, axis=0)
            m_old = m_ref[...]
            m_new = jnp.maximum(m_old, jnp.max(s, axis=-1, keepdims=True))
            alpha = jnp.exp2(m_old - m_new)
            p = jnp.exp2(s - m_new)
            l_ref[...] = alpha * l_ref[...] + jnp.sum(p, axis=-1, keepdims=True)
            m_ref[...] = m_new
            pb = p.astype(BF16)
            pv = [jnp.dot(pb[hh * sd:(hh + 1) * sd, :], values(hh), preferred_element_type=F32)
                  for hh in range(HEADS)]
            acc_ref[...] = alpha * acc_ref[...] + jnp.concatenate(pv, axis=0)

        def by_head(ref, n):
            return jnp.swapaxes(ref[...].astype(BF16).reshape(n, HEADS, HEAD_DIM), 0, 1)

        start = pl.multiple_of(kc * tk, tk)
        kk, vv = by_head(ck_ref, tk), by_head(cv_ref, tk)
        attend(lambda hh: kk[hh], lambda hh: vv[hh], lambda hh: cum_ref[hh:hh + 1, pl.ds(start, tk)], None)

        @pl.when(kc == nkc - 1)
        def _():
            row = lax.broadcasted_iota(jnp.int32, (sd, NEW_PAD), 0)
            col = lax.broadcasted_iota(jnp.int32, (sd, NEW_PAD), 1)
            nk, nv = by_head(nk_ref, sd), by_head(nv_ref, sd)
            attend(lambda hh: _pad_rows(nk[hh], NEW_PAD), lambda hh: _pad_rows(nv[hh], NEW_PAD),
                   lambda hh: cum_ref[hh:hh + 1, past - 1:past] + cn_ref[hh:hh + 1, :], col <= row)
            out = acc_ref[...] / l_ref[...]
            for hh in range(HEADS):
                gh = g_ref[:, hh * HEAD_DIM:(hh + 1) * HEAD_DIM].astype(F32)
                o_ref[:, hh * HEAD_DIM:(hh + 1) * HEAD_DIM] = (out[hh * sd:(hh + 1) * sd, :] * gh).astype(BF16)

    cache_spec = pl.BlockSpec((tk * HEADS, HEAD_DIM), lambda b, kc: ((layer * nb + b) * nkc + kc, 0))
    new_spec = pl.BlockSpec((None, sd * HEADS, HEAD_DIM), lambda b, kc: (layer, b, 0))
    return pl.pallas_call(
        body,
        out_shape=jax.ShapeDtypeStruct((ts, WIDTH), BF16),
        grid=(nb, nkc),
        in_specs=[
            pl.BlockSpec((sd, WIDTH), lambda b, kc: (b, 0)),
            cache_spec, cache_spec, new_spec, new_spec,
            pl.BlockSpec((None, None, HEADS, past), lambda b, kc: (layer, b, 0, 0)),
            pl.BlockSpec((None, HEADS, NEW_PAD), lambda b, kc: (b, 0, 0)),
            pl.BlockSpec((sd, WIDTH), lambda b, kc: (b, 0)),
        ],
        out_specs=pl.BlockSpec((sd, WIDTH), lambda b, kc: (b, 0)),
        scratch_shapes=[pltpu.VMEM((HEADS, past), F32), pltpu.VMEM((rows, 1), F32),
                        pltpu.VMEM((rows, 1), F32), pltpu.VMEM((rows, HEAD_DIM), F32)],
        compiler_params=_cparams("parallel", "arbitrary"),
        name=name,
    )(q, cache_k, cache_v, new_k, new_v, logf_t, cum_new, gate)


def _mla_decode_call(q_nope, q_pe, cache_ckv, cache_kpe_t, new_ckv, new_kpe2, layer, w_kvb, g_kn, gate,
                     nb, sd, past, name):
    ts = nb * sd
    rows = past + NEW_PAD
    hq = HEADS * sd
    per_head = 2 * HEAD_DIM

    def body(qn_ref, qp_ref, cc_ref, ck_ref, nc_ref, nk_ref, w_ref, g_ref, gate_ref, o_ref,
             lat_ref, spe_ref, p_ref, linv_ref, kk_ref):
        hp = pl.program_id(1)

        @pl.when(hp == 0)
        def _():
            lat_ref[0:past, :] = cc_ref[...].astype(BF16)
            lat_ref[past:rows, :] = _pad_rows(nc_ref[...], NEW_PAD)
            q_all = jnp.concatenate([qp_ref[:, h * LANES:(h + 1) * LANES] for h in range(HEADS)], axis=0)
            kp = ck_ref[...].astype(BF16)
            spe_ref[:, 0:past] = jnp.dot(q_all, jnp.concatenate([kp, kp], axis=0), preferred_element_type=F32)
            spe_ref[:, past:rows] = lax.dot_general(q_all, _pad_rows(nk_ref[...], NEW_PAD),
                                                    (((1,), (1,)), ((), ())), preferred_element_type=F32)

        def up_project(pair):
            col = pl.multiple_of(pair * 2 * per_head, 2 * per_head)
            wk = jnp.concatenate([w_ref[:, pl.ds(col, HEAD_DIM)], w_ref[:, pl.ds(col + per_head, HEAD_DIM)]],
                                 axis=1).astype(BF16)
            return jnp.dot(lat_ref[...], wk, preferred_element_type=F32)

        @pl.when(hp == 0)
        def _():
            kk_ref[...] = up_project(0)

        kk_next = up_project(jnp.minimum(hp + 1, HEADS // 2 - 1))
        kk = kk_ref[...]
        q_pos = past + lax.broadcasted_iota(jnp.int32, (sd, rows), 0)
        k_pos = lax.broadcasted_iota(jnp.int32, (sd, rows), 1)
        allowed = (k_pos // CHUNK <= q_pos // CHUNK) & (k_pos < past + sd)
        for g in range(2):
            hs = slice(g * HEAD_DIM, (g + 1) * HEAD_DIM)
            kn = (_rms(kk[:, hs]) * g_ref[...]).astype(BF16)
            r0 = pl.multiple_of((2 * hp + g) * sd, sd)
            s = lax.dot_general(qn_ref[:, hs], kn, (((1,), (1,)), ((), ())), preferred_element_type=F32)
            s = jnp.where(allowed, s + spe_ref[pl.ds(r0, sd), :], NEG_INF)
            p = jnp.exp2(s - jnp.max(s, axis=-1, keepdims=True))
            linv_ref[pl.ds(r0, sd), :] = 1.0 / jnp.sum(p, axis=-1, keepdims=True)
            p_ref[pl.ds(r0, sd), :] = p.astype(BF16)
        kk_ref[...] = kk_next

        @pl.when(hp == HEADS // 2 - 1)
        def _():
            ctx = jnp.dot(p_ref[...], lat_ref[...], preferred_element_type=F32)
            for h in range(HEADS):
                rs = slice(h * sd, (h + 1) * sd)
                wv = w_ref[:, h * per_head + HEAD_DIM:(h + 1) * per_head].astype(BF16)
                out = jnp.dot(ctx[rs, :].astype(BF16), wv, preferred_element_type=F32) * linv_ref[rs, :]
                hs = slice(h * HEAD_DIM, (h + 1) * HEAD_DIM)
                o_ref[:, hs] = (out * gate_ref[:, hs].astype(F32)).astype(BF16)

    return pl.pallas_call(
        body,
        out_shape=jax.ShapeDtypeStruct((ts, WIDTH), BF16),
        grid=(nb, HEADS // 2),
        in_specs=[
            pl.BlockSpec((sd, 2 * HEAD_DIM), lambda b, h: (b, h)),
            pl.BlockSpec((sd, WIDTH), lambda b, h: (b, 0)),
            pl.BlockSpec((None, past, KV_LORA), lambda b, h: (layer * nb + b, 0, 0)),
            pl.BlockSpec((None, ROPE_DIM, past), lambda b, h: (layer * nb + b, 0, 0)),
            pl.BlockSpec((sd, KV_LORA), lambda b, h: (b, 0)),
            pl.BlockSpec((sd, LANES), lambda b, h: (b, 0)),
            pl.BlockSpec((None, KV_LORA, HEADS * per_head), lambda b, h: (layer, 0, 0)),
            pl.BlockSpec((1, HEAD_DIM), lambda b, h: (0, 0)),
            pl.BlockSpec((sd, WIDTH), lambda b, h: (b, 1)),
        ],
        out_specs=pl.BlockSpec((sd, WIDTH), lambda b, h: (b, 0)),
        scratch_shapes=[pltpu.VMEM((rows, KV_LORA), BF16), pltpu.VMEM((hq, rows), F32),
                        pltpu.VMEM((hq, rows), BF16), pltpu.VMEM((hq, 1), F32),
                        pltpu.VMEM((rows, 2 * HEAD_DIM), F32)],
        compiler_params=_cparams("parallel", "arbitrary"),
        name=name,
    )(q_nope, q_pe, cache_ckv, cache_kpe_t, new_ckv, new_kpe2, w_kvb, g_kn, gate)


def _out_proj_call(o_a, o_b, w_out, layer, x, name):
    t, d = x.shape
    tm, tn = _row_tile(t, 1024), 512

    def body(a_ref, b_ref, wa_ref, wb_ref, x_ref, y_ref):
        acc = jnp.dot(a_ref[...], wa_ref[...].astype(BF16), preferred_element_type=F32)
        acc = acc + jnp.dot(b_ref[...], wb_ref[...].astype(BF16), preferred_element_type=F32)
        y_ref[...] = x_ref[...] + acc

    return pl.pallas_call(
        body,
        out_shape=jax.ShapeDtypeStruct((t, d), F32),
        grid=(t // tm, d // tn),
        in_specs=[
            pl.BlockSpec((tm, WIDTH), lambda i, j: (i, 0)),
            pl.BlockSpec((tm, WIDTH), lambda i, j: (i, 0)),
            pl.BlockSpec((None, WIDTH, tn), lambda i, j: (layer, 0, j)),
            pl.BlockSpec((None, WIDTH, tn), lambda i, j: (layer, 1, j)),
            pl.BlockSpec((tm, tn), lambda i, j: (i, j)),
        ],
        out_specs=pl.BlockSpec((tm, tn), lambda i, j: (i, j)),
        compiler_params=_cparams("parallel", "parallel"),
        name=name,
    )(o_a, o_b, w_out, w_out, x)


def _rope_tables(pos):
    half = ROPE_DIM // 2
    inv_freq = 1.0 / (ROPE_THETA ** (jnp.arange(half, dtype=F32) / half))
    ang = pos.astype(F32)[:, None] * inv_freq[None, :]
    cos, sin = jnp.cos(ang), jnp.sin(ang)
    cos64 = jnp.concatenate([cos, cos], axis=-1)
    sin64 = jnp.concatenate([-sin, sin], axis=-1)
    return jnp.concatenate([cos64, cos64], axis=-1), jnp.concatenate([sin64, sin64], axis=-1)


def _prep_w_in_call(w_in_t, layer, name):
    _, n_in, d = w_in_t.shape
    tc = _row_tile(d, 256)
    src = {}
    o = 0
    for seg, size in (("q", WIDTH), ("k", WIDTH), ("v", WIDTH), ("f", HEADS), ("za", WIDTH),
                      ("cq", Q_LORA), ("ckv", KV_LORA), ("kpe", ROPE_DIM), ("zb", WIDTH)):
        src[seg] = (o, size)
        o += size
    assert o == n_in
    dst = (("q", COL_Q), ("k", COL_K), ("v", COL_V), ("za", COL_GATE), ("zb", COL_GATE + WIDTH), ("cq", COL_CQ),
           ("ckv", COL_CKV), ("kpe", COL_SMALL), ("f", COL_SMALL + LANES))

    def body(x_ref, o_ref):
        for seg, d0 in dst:
            s0, size = src[seg]
            o_ref[d0:d0 + size, :] = x_ref[s0:s0 + size, :].astype(BF16)
        pad0 = COL_SMALL + ROPE_DIM
        o_ref[pad0:COL_SMALL + LANES, :] = jnp.zeros((LANES - ROPE_DIM, tc), BF16)
        pad1 = COL_SMALL + LANES + HEADS
        o_ref[pad1:COL_SMALL + N_SMALL, :] = jnp.zeros((LANES - HEADS, tc), BF16)

    return pl.pallas_call(
        body,
        out_shape=jax.ShapeDtypeStruct((COL_SMALL + N_SMALL, d), BF16),
        grid=(d // tc,),
        in_specs=[pl.BlockSpec((None, n_in, tc), lambda i: (layer, 0, i))],
        out_specs=pl.BlockSpec((COL_SMALL + N_SMALL, tc), lambda i: (0, i)),
        compiler_params=_cparams("parallel"),
        name=name,
    )(w_in_t)


def _prep_w_qb_call(w_qb, layer, name):
    _, k, n = w_qb.shape
    per_head = NOPE_DIM + ROPE_DIM
    tr = _row_tile(k, 256)

    def body(x_ref, o_ref):
        for hh in range(HEADS):
            o_ref[:, hh * NOPE_DIM:(hh + 1) * NOPE_DIM] = x_ref[:, hh * per_head:hh * per_head + NOPE_DIM].astype(BF16)
        for p in range(HEADS // 2):
            pe = [x_ref[:, hh * per_head + NOPE_DIM:(hh + 1) * per_head] for hh in (2 * p, 2 * p + 1)]
            o_ref[:, WIDTH + p * LANES:WIDTH + (p + 1) * LANES] = jnp.concatenate(pe, axis=1).astype(BF16)

    return pl.pallas_call(
        body,
        out_shape=jax.ShapeDtypeStruct((k, n), BF16),
        grid=(k // tr,),
        in_specs=[pl.BlockSpec((None, tr, n), lambda i: (layer, i, 0))],
        out_specs=pl.BlockSpec((tr, n), lambda i: (i, 0)),
        compiler_params=_cparams("parallel"),
        name=name,
    )(w_qb)


def _layer_params(l, g_norm, b_f, g_q_fox, g_k_fox, g_cq, g_qn, g_qp, g_ckv, g_kp, g_kn):
    row = lambda a: a.reshape(1, -1).astype(F32)
    pad = lambda a: jnp.concatenate([row(a), jnp.zeros((1, LANES - a.shape[-1]), F32)], axis=1)
    return dict(
        g_norm=row(g_norm[l]), b_f_row=pad(b_f[l]), g_q=row(g_q_fox[l]), g_k=row(g_k_fox[l]), g_cq=row(g_cq[l]),
        g_qn=row(g_qn[l]), g_qp2=jnp.concatenate([row(g_qp[l]), row(g_qp[l])], axis=1),
        g_ckv=row(g_ckv[l]), g_kp=pad(g_kp[l]), g_kn=row(g_kn[l]),
    )


def _input_stage(x, w, w_qb, p, cos, sin, layer, prev, depth, nb, sl, prompt, tag):
    h = _rmsnorm_call(x, p["g_norm"], f"norm_{tag}")
    prev = prev or {}
    r = {}
    r["q"] = _proj_heads(h, w, COL_Q, p["g_q"], FOX_SCALE, nb, sl, prompt, f"proj_q_{tag}",
                         w_rows_are_outputs=True)
    k_out = _proj_kv_fox(h, w, COL_K, p["g_k"], layer, prev.get("k"), depth, nb, sl, prompt, f"proj_k_{tag}")
    v_out = _proj_kv_fox(h, w, COL_V, None, layer, prev.get("v"), depth, nb, sl, prompt, f"proj_v_{tag}")
    r["k"], r["v"] = k_out[0], v_out[0]
    if prompt:
        r["k_bf"], r["v_bf"] = k_out[1], v_out[1]
    r["gate"] = _proj_gate(h, w, f"proj_gate_{tag}")
    cq = _proj_cq(h, w, p["g_cq"], f"proj_cq_{tag}")
    r["ckv"], r["ckv_bf"] = _proj_ckv(h, w, p["g_ckv"], layer, prev.get("ckv"), depth, f"proj_ckv_{tag}")
    r["kpe"], r["kpe2"], r["logf"], r["cum"] = _proj_small(
        h, w, p["g_kp"], p["b_f_row"], cos, sin, layer, prev.get("kpe"), prev.get("logf"), depth, nb, sl, prompt,
        f"proj_small_{tag}")
    r["q_nope"] = _proj_heads(cq, w_qb, 0, p["g_qn"], MLA_SCALE, nb, sl, prompt, f"proj_qn_{tag}")
    r["q_pe"] = _proj_q_pe(cq, w_qb, p["g_qp2"], cos, sin, nb, sl, prompt, f"proj_qp_{tag}")
    return r


def kernel(x_prompt, x_sample, cache_fox_k, cache_fox_v, cache_fox_logf, cache_mla_ckv, cache_mla_kpe, g_norm,
           w_in, b_f, g_q_fox, g_k_fox, g_cq, w_qb, g_qn, g_qp, g_ckv, g_kp, w_kvb, g_kn, w_out):
    nb, sl, d = x_prompt.shape
    nbd, sd, _ = x_sample.shape
    depth = w_in.shape[0]
    past = cache_fox_k.shape[2]
    assert sl % CHUNK == 0 and past % CHUNK == 0 and sd <= CHUNK, "chunk-aligned streaming shapes only"

    cos_p, sin_p = _rope_tables(jnp.arange(sl, dtype=jnp.int32))
    cos_p, sin_p = jnp.tile(cos_p, (nb, 1)), jnp.tile(sin_p, (nb, 1))
    cos_s, sin_s = _rope_tables(past + jnp.arange(sd, dtype=jnp.int32))
    cos_s, sin_s = jnp.tile(cos_s, (nbd, 1)), jnp.tile(sin_s, (nbd, 1))

    ck2 = cache_fox_k.reshape(-1, HEAD_DIM)
    cv2 = cache_fox_v.reshape(-1, HEAD_DIM)
    logf_t = jnp.swapaxes(cache_fox_logf, 2, 3)
    w_in_t = jnp.swapaxes(w_in, 1, 2)
    cckv = cache_mla_ckv.reshape(depth * nbd, past, KV_LORA)
    ckpe = jnp.swapaxes(cache_mla_kpe, 2, 3).reshape(depth * nbd, ROPE_DIM, past)

    y_p = x_prompt.reshape(nb * sl, d)
    y_s = x_sample.reshape(nbd * sd, d)
    prev_p, prev_s = None, None
    for l in range(depth):
        p = _layer_params(l, g_norm, b_f, g_q_fox, g_k_fox, g_cq, g_qn, g_qp, g_ckv, g_kp, g_kn)
        w = _prep_w_in_call(w_in_t, l, f"prep_w_in_{l}")
        wq = _prep_w_qb_call(w_qb, l, f"prep_w_qb_{l}")
        rp = _input_stage(y_p, w, wq, p, cos_p, sin_p, l, prev_p, depth, nb, sl, True, f"p{l}")
        kn_p, v_p = _proj_kv_up(rp["ckv_bf"], w_kvb, l, p["g_kn"], nb, sl, f"kv_up_p{l}")
        o_a = _flash_call([rp["q"]], rp["k_bf"], rp["cum"].reshape(nb, sl, LANES), rp["v_bf"], rp["gate"], 0,
                          nb, sl, True, f"fox_attn_p{l}")
        o_b = _flash_call([rp["q_nope"], rp["q_pe"]], kn_p, rp["kpe2"].reshape(nb, sl, LANES), v_p, rp["gate"],
                          HEADS, nb, sl, False, f"mla_attn_p{l}")
        y_p = _out_proj_call(o_a, o_b, w_out, l, y_p, f"out_proj_p{l}")
        prev_p = rp
        rs = _input_stage(y_s, w, wq, p, cos_s, sin_s, l, prev_s, depth, nbd, sd, False, f"s{l}")
        o_a = _fox_decode_call(rs["q"], ck2, cv2, rs["k"], rs["v"], l, logf_t, rs["cum"], rs["gate"], nbd, sd,
                               past, f"fox_attn_s{l}")
        o_b = _mla_decode_call(rs["q_nope"], rs["q_pe"], cckv, ckpe, rs["ckv_bf"], rs["kpe2"], l, w_kvb,
                               p["g_kn"], rs["gate"], nbd, sd, past, f"mla_attn_s{l}")
        y_s = _out_proj_call(o_a, o_b, w_out, l, y_s, f"out_proj_s{l}")
        prev_s = rs

    hk = (HEADS, HEAD_DIM)
    return (y_p.reshape(nb, sl, d), y_s.reshape(nbd, sd, d),
            prev_p["k"].reshape((depth, nb, sl) + hk), prev_p["v"].reshape((depth, nb, sl) + hk),
            prev_p["logf"].reshape(depth, nb, sl, HEADS), prev_p["ckv"].reshape(depth, nb, sl, KV_LORA),
            prev_p["kpe"].reshape(depth, nb, sl, ROPE_DIM),
            prev_s["k"].reshape((depth, nbd, sd) + hk), prev_s["v"].reshape((depth, nbd, sd) + hk),
            prev_s["logf"].reshape(depth, nbd, sd, HEADS), prev_s["ckv"].reshape(depth, nbd, sd, KV_LORA),
            prev_s["kpe"].reshape(depth, nbd, sd, ROPE_DIM))
```

```python
import math

import jax
import jax.numpy as jnp
from jax import lax
from jax.experimental import pallas as pl
from jax.experimental.pallas import tpu as pltpu

F32 = jnp.float32
BF16 = jnp.bfloat16

CHUNK = 64
HEADS = 16
HEAD_DIM = 128
WIDTH = HEADS * HEAD_DIM
NOPE_DIM = 128
ROPE_DIM = 64
Q_LORA = 1024
KV_LORA = 512
ROPE_THETA = 10000.0
EPS = 1e-6
LOG2E = math.log2(math.e)
FOX_SCALE = LOG2E / math.sqrt(HEAD_DIM)
MLA_SCALE = LOG2E / math.sqrt(NOPE_DIM + ROPE_DIM)
LANES = 128
SUBLANES = 8
NEW_PAD = 128
VMEM_LIMIT_BYTES = 52 * 1024 * 1024
NEG_INF = float("-inf")

COL_Q, COL_K, COL_V, COL_GATE = 0, WIDTH, 2 * WIDTH, 3 * WIDTH
COL_CQ = 5 * WIDTH
COL_CKV = COL_CQ + Q_LORA
COL_SMALL = COL_CKV + KV_LORA
N_SMALL = 2 * LANES


def _cparams(*sem):
    return pltpu.CompilerParams(dimension_semantics=sem, vmem_limit_bytes=VMEM_LIMIT_BYTES)


def _row_tile(t, cap):
    if t <= cap:
        return t
    for c in range(cap, 15, -16):
        if t % c == 0:
            return c
    return t


def _rms(x, n=None, axis=-1):
    n = x.shape[axis] if n is None else n
    ms = jnp.sum(x * x, axis=axis, keepdims=True) * (1.0 / n)
    return x * lax.rsqrt(ms + EPS)


def _silu(x):
    return x * (1.0 / (1.0 + jnp.exp(-x)))


def _log_sigmoid(x):
    return jnp.minimum(x, 0.0) - jnp.log1p(jnp.exp(-jnp.abs(x)))


def _scan(x, axis, length, pos):
    sh = 1
    while sh < length:
        x = x + jnp.where(pos >= sh, pltpu.roll(x, sh, axis), 0.0)
        sh *= 2
    return x


def _rmsnorm_call(x, g, name):
    t, d = x.shape
    tm = _row_tile(t, 256)

    def body(x_ref, g_ref, o_ref):
        o_ref[...] = (_rms(x_ref[...]) * g_ref[...]).astype(BF16)

    return pl.pallas_call(
        body,
        out_shape=jax.ShapeDtypeStruct((t, d), BF16),
        grid=(t // tm,),
        in_specs=[pl.BlockSpec((tm, d), lambda i: (i, 0)), pl.BlockSpec((1, d), lambda i: (0, 0))],
        out_specs=pl.BlockSpec((tm, d), lambda i: (i, 0)),
        compiler_params=_cparams("parallel"),
        name=name,
    )(x, g)


def _proj_call(x, w, col0, n, tm, tn, extras, extra_specs, out_shapes, out_specs, epilogue, name,
               alias_inputs=(), scratch_shapes=(), layer=None, w_rows_are_outputs=False, row_chunks=1,
               whole_tile_epilogue=False):
    t, k = x.shape
    assert col0 % tn == 0 and n % tn == 0 and t % tm == 0
    cb0 = col0 // tn
    n_ex, n_al, n_out = len(extras), len(alias_inputs), len(out_shapes)
    contract = (((1,), (1,)), ((), ())) if w_rows_are_outputs else (((1,), (0,)), ((), ()))

    assert tm % (16 * row_chunks) == 0
    nr = tm // row_chunks

    def body(x_ref, w_ref, *refs):
        ex = refs[:n_ex]
        outs = refs[n_ex + n_al:n_ex + n_al + n_out]
        scr = refs[n_ex + n_al + n_out:]
        wv = w_ref[...].astype(BF16)
        accs = []
        for c in range(row_chunks):
            rows = slice(c * nr, (c + 1) * nr)
            acc = lax.dot_general(x_ref[rows, :], wv, contract, preferred_element_type=F32)
            if whole_tile_epilogue:
                accs.append(acc)
            else:
                epilogue(acc, rows, x_ref, ex, outs, scr)
        if whole_tile_epilogue:
            epilogue(jnp.concatenate(accs, axis=0), slice(0, tm), x_ref, ex, outs, scr)

    if w_rows_are_outputs:
        assert layer is None
        w_spec = pl.BlockSpec((tn, k), lambda i, j: (cb0 + j, 0))
    elif layer is None:
        w_spec = pl.BlockSpec((k, tn), lambda i, j: (0, cb0 + j))
    else:
        w_spec = pl.BlockSpec((None, k, tn), lambda i, j: (layer, 0, cb0 + j))
    in_specs = [pl.BlockSpec((tm, k), lambda i, j: (i, 0)), w_spec]
    in_specs += list(extra_specs)
    in_specs += [pl.BlockSpec(memory_space=pl.ANY) for _ in alias_inputs]
    aliases = {2 + n_ex + a: out_idx for a, (_, out_idx) in enumerate(alias_inputs)}
    return pl.pallas_call(
        body,
        out_shape=out_shapes,
        grid=(t // tm, n // tn),
        in_specs=in_specs,
        out_specs=out_specs,
        scratch_shapes=list(scratch_shapes),
        input_output_aliases=aliases,
        compiler_params=_cparams("arbitrary", "arbitrary"),
        name=name,
    )(x, w, *extras, *[a for a, _ in alias_inputs])


def _chunks(tm):
    return 2 if tm % 512 == 0 else 1


def _const_spec(shape):
    nd = len(shape)
    return pl.BlockSpec(shape, lambda i, j: (0,) * nd)


def _head_out(nb, sl, tm, nh_t, head_major):
    if head_major:
        nsb = sl // tm
        shape = jax.ShapeDtypeStruct((nb, HEADS, sl, HEAD_DIM), BF16)
        spec = pl.BlockSpec((None, nh_t, tm, HEAD_DIM), lambda i, j: (i // nsb, j, i % nsb, 0))
    else:
        shape = jax.ShapeDtypeStruct((nb * sl, WIDTH), BF16)
        spec = pl.BlockSpec((tm, nh_t * HEAD_DIM), lambda i, j: (i, j))
    return shape, spec


def _store_head(o_ref, hh, rows, val, head_major):
    if head_major:
        o_ref[hh, rows, :] = val
    else:
        o_ref[rows, hh * HEAD_DIM:(hh + 1) * HEAD_DIM] = val


def _proj_heads(h, w, col0, g, scale, nb, sl, head_major, name, w_rows_are_outputs=False):
    t, k = h.shape
    tm, tn = _row_tile(min(t, sl) if head_major else t, 1024), (1024 if k <= 1024 else 512)
    nh_t = tn // HEAD_DIM
    shape, spec = _head_out(nb, sl, tm, nh_t, head_major)

    def epi(acc, rows, x_ref, ex, outs, scr):
        g_ref, = ex
        for hh in range(nh_t):
            v = _rms(acc[:, hh * HEAD_DIM:(hh + 1) * HEAD_DIM]) * g_ref[...] * scale
            _store_head(outs[0], hh, rows, v.astype(BF16), head_major)

    return _proj_call(h, w, col0, WIDTH, tm, tn, [g], [_const_spec((1, HEAD_DIM))], [shape], [spec], epi, name,
                      w_rows_are_outputs=w_rows_are_outputs, row_chunks=_chunks(tm))[0]


def _proj_kv_fox(h, w, col0, g, layer, prev, depth, nb, sl, head_major, name):
    t = h.shape[0]
    tm, tn = _row_tile(min(t, sl) if head_major else t, 1024), 512
    nh_t = tn // HEAD_DIM
    f_shape = jax.ShapeDtypeStruct((depth, t * HEADS, HEAD_DIM), F32)
    f_spec = pl.BlockSpec((None, tm * HEADS, HEAD_DIM), lambda i, j: (layer, i, 0))
    shapes, specs = [f_shape], [f_spec]
    if head_major:
        b_shape, b_spec = _head_out(nb, sl, tm, nh_t, True)
        shapes.append(b_shape)
        specs.append(b_spec)
    extras = [] if g is None else [g]
    especs = [] if g is None else [_const_spec((1, HEAD_DIM))]

    def epi(acc, rows, x_ref, ex, outs, scr):
        j = pl.program_id(1)
        for hh in range(nh_t):
            v = acc[:, hh * HEAD_DIM:(hh + 1) * HEAD_DIM]
            if g is not None:
                v = _rms(v) * ex[0][...]
            nr = rows.stop - rows.start
            outs[0][pl.ds(rows.start * HEADS + j * nh_t + hh, nr, stride=HEADS), :] = v
            if head_major:
                outs[1][hh, rows, :] = v.astype(BF16)

    alias = [] if prev is None else [(prev, 0)]
    return _proj_call(h, w, col0, WIDTH, tm, tn, extras, especs, shapes, specs, epi, name, alias_inputs=alias,
                      w_rows_are_outputs=True, row_chunks=_chunks(tm))


def _proj_gate(h, w, name):
    t = h.shape[0]
    tm, tn = _row_tile(t, 1024), 512

    def epi(acc, rows, x_ref, ex, outs, scr):
        outs[0][rows, :] = _silu(acc).astype(BF16)

    return _proj_call(h, w, COL_GATE, 2 * WIDTH, tm, tn, [], [], [jax.ShapeDtypeStruct((t, 2 * WIDTH), BF16)],
                      [pl.BlockSpec((tm, tn), lambda i, j: (i, j))], epi, name, w_rows_are_outputs=True,
                      row_chunks=_chunks(tm))[0]


def _proj_cq(h, w, g, name):
    t = h.shape[0]
    tm, tn = _row_tile(t, 1024), Q_LORA

    def epi(acc, rows, x_ref, ex, outs, scr):
        outs[0][rows, :] = (_rms(acc) * ex[0][...]).astype(BF16)

    return _proj_call(h, w, COL_CQ, Q_LORA, tm, tn, [g], [_const_spec((1, Q_LORA))],
                      [jax.ShapeDtypeStruct((t, Q_LORA), BF16)], [pl.BlockSpec((tm, tn), lambda i, j: (i, 0))],
                      epi, name, w_rows_are_outputs=True, row_chunks=_chunks(tm))[0]


def _proj_ckv(h, w, g, layer, prev, depth, name):
    t = h.shape[0]
    tm, tn = _row_tile(t, 1024), KV_LORA

    def epi(acc, rows, x_ref, ex, outs, scr):
        v = _rms(acc) * ex[0][...]
        outs[0][rows, :] = v
        outs[1][rows, :] = v.astype(BF16)

    shapes = [jax.ShapeDtypeStruct((depth, t, KV_LORA), F32), jax.ShapeDtypeStruct((t, KV_LORA), BF16)]
    specs = [pl.BlockSpec((None, tm, tn), lambda i, j: (layer, i, 0)), pl.BlockSpec((tm, tn), lambda i, j: (i, 0))]
    alias = [] if prev is None else [(prev, 0)]
    return _proj_call(h, w, COL_CKV, KV_LORA, tm, tn, [g], [_const_spec((1, KV_LORA))], shapes, specs, epi, name,
                      alias_inputs=alias, w_rows_are_outputs=True, row_chunks=_chunks(tm))


def _swap_halves(x):
    lane = lax.broadcasted_iota(jnp.int32, x.shape, 1)
    return jnp.where((lane % ROPE_DIM) < ROPE_DIM // 2, pltpu.roll(x, LANES - ROPE_DIM // 2, 1),
                     pltpu.roll(x, ROPE_DIM // 2, 1))


def _proj_small(h, w, g_kp, b_f_row, cos, sin, layer, prev_kpe, prev_logf, depth, nb, sl, prompt, name):
    t = h.shape[0]
    tm = _row_tile(min(t, sl), 1024) if prompt else t
    nsb = sl // tm if prompt else 1
    seqs_per_blk = 1 if prompt else tm // sl

    def epi(acc, rows, x_ref, ex, outs, scr):
        g_ref, bfr_ref, cos_ref, sin_ref = ex
        kpe_ref, kpe2_ref, logf_ref, cum_ref = outs
        i = pl.program_id(0)
        kp = _rms(acc[:, :LANES], ROPE_DIM) * g_ref[...]
        kp = kp * cos_ref[...] + _swap_halves(kp) * sin_ref[...]
        kpe_ref[...] = kp[:, :ROPE_DIM]
        kpe2_ref[...] = (kp + pltpu.roll(kp, ROPE_DIM, 1)).astype(BF16)
        lf = _log_sigmoid(acc[:, LANES:2 * LANES] + bfr_ref[...])
        logf_ref[...] = lf[:, :HEADS]
        if prompt:
            carry_ref, = scr
            lane = lax.broadcasted_iota(jnp.int32, lf.shape, 1)
            row = lax.broadcasted_iota(jnp.int32, lf.shape, 0)
            c = _scan(jnp.where(lane < HEADS, lf, 0.0), 0, tm, row)
            if nsb > 1:
                @pl.when(i % nsb == 0)
                def _():
                    carry_ref[...] = jnp.zeros_like(carry_ref)
                c = c + carry_ref[0:1, :]
                carry_ref[...] = jnp.broadcast_to(c[tm - 1:tm, :], carry_ref.shape)
            neg = c * (-LOG2E)
            hi = neg.astype(BF16).astype(F32)
            mid = (neg - hi).astype(BF16).astype(F32)
            lo = neg - hi - mid
            cum_ref[...] = (hi + pltpu.roll(mid, HEADS, 1) + pltpu.roll(lo, 2 * HEADS, 1)).astype(BF16)
        else:
            c = lf.T[:HEADS, :]
            pos = lax.broadcasted_iota(jnp.int32, c.shape, 1) % sl
            c = _scan(c, 1, sl, pos) * (-LOG2E)
            cum_ref[...] = jnp.zeros(cum_ref.shape, F32)
            for s in range(seqs_per_blk):
                cum_ref[s, :, 0:sl] = c[:, s * sl:(s + 1) * sl]

    if prompt:
        cum_shape = jax.ShapeDtypeStruct((t, LANES), BF16)
        cum_spec = pl.BlockSpec((tm, LANES), lambda i, j: (i, 0))
        scratch = [pltpu.VMEM((SUBLANES, LANES), F32)]
    else:
        assert sl <= LANES
        cum_shape = jax.ShapeDtypeStruct((nb, HEADS, LANES), F32)
        cum_spec = pl.BlockSpec((seqs_per_blk, HEADS, LANES), lambda i, j: (i, 0, 0))
        scratch = []
    shapes = [jax.ShapeDtypeStruct((depth, t, ROPE_DIM), F32), jax.ShapeDtypeStruct((t, LANES), BF16),
              jax.ShapeDtypeStruct((depth, t, HEADS), F32), cum_shape]
    specs = [pl.BlockSpec((None, tm, ROPE_DIM), lambda i, j: (layer, i, 0)),
             pl.BlockSpec((tm, LANES), lambda i, j: (i, 0)),
             pl.BlockSpec((None, tm, HEADS), lambda i, j: (layer, i, 0)), cum_spec]
    extras = [g_kp, b_f_row, cos, sin]
    especs = [_const_spec((1, LANES)), _const_spec((1, LANES)),
              pl.BlockSpec((tm, LANES), lambda i, j: (i, 0)), pl.BlockSpec((tm, LANES), lambda i, j: (i, 0))]
    alias = []
    if prev_kpe is not None:
        alias = [(prev_kpe, 0), (prev_logf, 2)]
    return _proj_call(h, w, COL_SMALL, N_SMALL, tm, N_SMALL, extras, especs, shapes, specs, epi, name,
                      alias_inputs=alias, scratch_shapes=scratch, w_rows_are_outputs=True,
                      row_chunks=_chunks(tm), whole_tile_epilogue=True)


def _proj_q_pe(cq, w, g2, cos, sin, nb, sl, head_major, name):
    t = cq.shape[0]
    tm, tn = _row_tile(min(t, sl) if head_major else t, 512), HEADS * ROPE_DIM
    shape, spec = _head_out(nb, sl, tm, HEADS, head_major)

    def epi(acc, rows, x_ref, ex, outs, scr):
        g_ref, cos_ref, sin_ref = ex
        lane = lax.broadcasted_iota(jnp.int32, (acc.shape[0], LANES), 1)
        lo = lane < ROPE_DIM
        cos_t, sin_t = cos_ref[rows, :], sin_ref[rows, :]
        for p in range(HEADS // 2):
            x = acc[:, p * LANES:(p + 1) * LANES]
            x2 = x * x
            ms_lo = jnp.sum(jnp.where(lo, x2, 0.0), axis=-1, keepdims=True)
            ms_hi = jnp.sum(jnp.where(lo, 0.0, x2), axis=-1, keepdims=True)
            ms = jnp.where(lo, ms_lo, ms_hi) * (1.0 / ROPE_DIM)
            y = x * lax.rsqrt(ms + EPS) * g_ref[...]
            y = (y * cos_t + _swap_halves(y) * sin_t) * MLA_SCALE
            _store_head(outs[0], 2 * p, rows, jnp.where(lo, y, 0.0).astype(BF16), head_major)
            _store_head(outs[0], 2 * p + 1, rows, jnp.where(lo, 0.0, y).astype(BF16), head_major)

    especs = [_const_spec((1, LANES)), pl.BlockSpec((tm, LANES), lambda i, j: (i, 0)),
              pl.BlockSpec((tm, LANES), lambda i, j: (i, 0))]
    return _proj_call(cq, w, WIDTH, tn, tm, tn, [g2, cos, sin], especs, [shape], [spec], epi, name,
                      row_chunks=_chunks(tm))[0]


def _proj_kv_up(ckv, w, layer, g, nb, sl, name):
    t = ckv.shape[0]
    tm, tn = _row_tile(min(t, sl), 1024), 1024
    nh_t = tn // (2 * HEAD_DIM)
    nsb = sl // tm
    shape = jax.ShapeDtypeStruct((nb, HEADS, sl, HEAD_DIM), BF16)
    spec = pl.BlockSpec((None, nh_t, tm, HEAD_DIM), lambda i, j: (i // nsb, j, i % nsb, 0))

    def epi(acc, rows, x_ref, ex, outs, scr):
        for hh in range(nh_t):
            base = hh * 2 * HEAD_DIM
            outs[0][hh, rows, :] = (_rms(acc[:, base:base + HEAD_DIM]) * ex[0][...]).astype(BF16)
            outs[1][hh, rows, :] = acc[:, base + HEAD_DIM:base + 2 * HEAD_DIM].astype(BF16)

    return _proj_call(ckv, w, 0, 2 * WIDTH, tm, tn, [g], [_const_spec((1, HEAD_DIM))], [shape, shape],
                      [spec, spec], epi, name, layer=layer, row_chunks=_chunks(tm))


FLASH_HEADS_PER_STEP = 2


def _flash_call(q_parts, k_head, k_shared, v, gate, gate_col0, nb, sl, fox, name):
    tq = _row_tile(sl, 256)
    nqb = sl // tq
    t = nb * sl
    n_q = len(q_parts)

    hps = FLASH_HEADS_PER_STEP
    chains = [(hh, qi) for hh in range(hps) for qi in range(nqb)]

    def body(*refs):
        q_refs = refs[:n_q]
        kh_ref, ks_ref, v_ref, g_ref, o_ref = refs[n_q:]
        hp = pl.program_id(1)
        key = lax.broadcasted_iota(jnp.int32, (tq, tq), 0)
        qry = lax.broadcasted_iota(jnp.int32, (tq, tq), 1)
        allowed = (key <= qry) if fox else (key // CHUNK <= qry // CHUNK)
        if fox:
            lane = lax.broadcasted_iota(jnp.int32, (tq, LANES), 1)
            selectors = [jnp.where((lane % HEADS == hp * hps + hh) & (lane < 3 * HEADS), 1.0, 0.0).astype(BF16)
                         for hh in range(hps)]

        def q_block(hh, qi):
            rows = slice(qi * tq, (qi + 1) * tq)
            parts = [r[hh, rows, :] for r in q_refs] + ([selectors[hh]] if fox else [])
            return jnp.concatenate(parts, axis=-1)

        def k_block(hh, kb):
            rows = slice(kb * tq, (kb + 1) * tq)
            return jnp.concatenate([kh_ref[hh, rows, :], ks_ref[rows, :]], axis=-1)

        def score(rnd, hh, qi):
            s = lax.dot_general(k_block(hh, qi - rnd), q_block(hh, qi), (((1,), (1,)), ((), ())),
                                preferred_element_type=F32)
            return jnp.where(allowed, s, NEG_INF) if rnd == 0 else s

        state = {c: (jnp.full((1, tq), NEG_INF, F32), jnp.zeros((1, tq), F32), jnp.zeros((HEAD_DIM, tq), F32))
                 for c in chains}
        s_cur = {c: score(0, *c) for c in chains}
        for rnd in range(nqb):
            s_next = {}
            for hh, qi in chains:
                if qi < rnd:
                    continue
                if qi >= rnd + 1 and rnd + 1 < nqb:
                    s_next[hh, qi] = score(rnd + 1, hh, qi)
                m, l, acc = state[hh, qi]
                s = s_cur[hh, qi]
                m_new = jnp.maximum(m, jnp.max(s, axis=0, keepdims=True))
                alpha = jnp.exp2(m - m_new)
                p = jnp.exp2(s - m_new)
                l = alpha * l + jnp.sum(p, axis=0, keepdims=True)
                kb = qi - rnd
                pv = lax.dot_general(v_ref[hh, kb * tq:(kb + 1) * tq, :], p.astype(BF16),
                                     (((0,), (0,)), ((), ())), preferred_element_type=F32)
                state[hh, qi] = (m_new, l, alpha * acc + pv)
            s_cur = s_next
        for hh, qi in chains:
            m, l, acc = state[hh, qi]
            rows = slice(qi * tq, (qi + 1) * tq)
            cols = slice(hh * HEAD_DIM, (hh + 1) * HEAD_DIM)
            o_ref[rows, cols] = ((acc / l).T * g_ref[rows, cols].astype(F32)).astype(BF16)

    head_spec = pl.BlockSpec((None, hps, sl, HEAD_DIM), lambda b, h: (b, h, 0, 0))
    gate_cb0 = gate_col0 // hps
    in_specs = [head_spec for _ in q_parts] + [
        head_spec, pl.BlockSpec((None, sl, LANES), lambda b, h: (b, 0, 0)), head_spec,
        pl.BlockSpec((None, sl, hps * HEAD_DIM), lambda b, h: (b, 0, gate_cb0 + h))]
    return pl.pallas_call(
        body,
        out_shape=jax.ShapeDtypeStruct((nb, sl, WIDTH), BF16),
        grid=(nb, HEADS // hps),
        in_specs=in_specs,
        out_specs=pl.BlockSpec((None, sl, hps * HEAD_DIM), lambda b, h: (b, 0, h)),
        compiler_params=_cparams("parallel", "parallel"),
        name=name,
    )(*q_parts, k_head, k_shared, v, gate.reshape(nb, sl, 2 * WIDTH)).reshape(t, WIDTH)


def _pad_rows(x, rows):
    return jnp.concatenate([x, jnp.zeros((rows - x.shape[0], x.shape[1]), x.dtype)], axis=0)


def _fox_decode_call(q, cache_k, cache_v, new_k, new_v, layer, logf_t, cum_new, gate, nb, sd, past, name):
    tk = _row_tile(past, 512)
    nkc = past // tk
    ts = nb * sd
    rows = HEADS * sd

    def body(q_ref, ck_ref, cv_ref, nk_ref, nv_ref, lf_ref, cn_ref, g_ref, o_ref, cum_ref, m_ref, l_ref, acc_ref):
        kc = pl.program_id(1)

        @pl.when(kc == 0)
        def _():
            c = lf_ref[...]
            cum_ref[...] = _scan(c, 1, past, lax.broadcasted_iota(jnp.int32, c.shape, 1)) * (-LOG2E)
            m_ref[...] = jnp.full(m_ref.shape, NEG_INF, F32)
            l_ref[...] = jnp.zeros(l_ref.shape, F32)
            acc_ref[...] = jnp.zeros(acc_ref.shape, F32)

        def attend(keys, values, bias, mask):
            s = []
            for hh in range(HEADS):
                qh = q_ref[:, hh * HEAD_DIM:(hh + 1) * HEAD_DIM]
                sh = lax.dot_general(qh, keys(hh), (((1,), (1,)), ((), ())), preferred_element_type=F32)
                sh = sh + bias(hh)
                s.append(sh if mask is None else jnp.where(mask, sh, NEG_INF))
            s = jnp.concatenate(s, axis=0)
            m_old = m_ref[...]
            m_new = jnp.maximum(m_old, jnp.max(s, axis=-1, keepdims=True))
            alpha = jnp.exp2(m_old - m_new)
            p = jnp.exp2(s - m_new)
            l_ref[...] = alpha * l_ref[...] + jnp.sum(p, axis=-1, keepdims=True)
            m_ref[...] = m_new
            pb = p.astype(BF16)
            pv = [jnp.dot(pb[hh * sd:(hh + 1) * sd, :], values(hh), preferred_element_type=F32)
                  for hh in range(HEADS)]
            acc_ref[...] = alpha * acc_ref[...] + jnp.concatenate(pv, axis=0)

        def by_head(ref, n):
            return jnp.swapaxes(ref[...].astype(BF16).reshape(n, HEADS, HEAD_DIM), 0, 1)

        start = pl.multiple_of(kc * tk, tk)
        kk, vv = by_head(ck_ref, tk), by_head(cv_ref, tk)
        attend(lambda hh: kk[hh], lambda hh: vv[hh], lambda hh: cum_ref[hh:hh + 1, pl.ds(start, tk)], None)

        @pl.when(kc == nkc - 1)
        def _():
            row = lax.broadcasted_iota(jnp.int32, (sd, NEW_PAD), 0)
            col = lax.broadcasted_iota(jnp.int32, (sd, NEW_PAD), 1)
            nk, nv = by_head(nk_ref, sd), by_head(nv_ref, sd)
            attend(lambda hh: _pad_rows(nk[hh], NEW_PAD), lambda hh: _pad_rows(nv[hh], NEW_PAD),
                   lambda hh: cum_ref[hh:hh + 1, past - 1:past] + cn_ref[hh:hh + 1, :], col <= row)
            out = acc_ref[...] / l_ref[...]
            for hh in range(HEADS):
                gh = g_ref[:, hh * HEAD_DIM:(hh + 1) * HEAD_DIM].astype(F32)
                o_ref[:, hh * HEAD_DIM:(hh + 1) * HEAD_DIM] = (out[hh * sd:(hh + 1) * sd, :] * gh).astype(BF16)

    cache_spec = pl.BlockSpec((tk * HEADS, HEAD_DIM), lambda b, kc: ((layer * nb + b) * nkc + kc, 0))
    new_spec = pl.BlockSpec((None, sd * HEADS, HEAD_DIM), lambda b, kc: (layer, b, 0))
    return pl.pallas_call(
        body,
        out_shape=jax.ShapeDtypeStruct((ts, WIDTH), BF16),
        grid=(nb, nkc),
        in_specs=[
            pl.BlockSpec((sd, WIDTH), lambda b, kc: (b, 0)),
            cache_spec, cache_spec, new_spec, new_spec,
            pl.BlockSpec((None, None, HEADS, past), lambda b, kc: (layer, b, 0, 0)),
            pl.BlockSpec((None, HEADS, NEW_PAD), lambda b, kc: (b, 0, 0)),
            pl.BlockSpec((sd, WIDTH), lambda b, kc: (b, 0)),
        ],
        out_specs=pl.BlockSpec((sd, WIDTH), lambda b, kc: (b, 0)),
        scratch_shapes=[pltpu.VMEM((HEADS, past), F32), pltpu.VMEM((rows, 1), F32),
                        pltpu.VMEM((rows, 1), F32), pltpu.VMEM((rows, HEAD_DIM), F32)],
        compiler_params=_cparams("parallel", "arbitrary"),
        name=name,
    )(q, cache_k, cache_v, new_k, new_v, logf_t, cum_new, gate)


def _mla_decode_call(q_nope, q_pe, cache_ckv, cache_kpe_t, new_ckv, new_kpe2, layer, w_kvb, g_kn, gate,
                     nb, sd, past, name):
    ts = nb * sd
    rows = past + NEW_PAD
    hq = HEADS * sd
    per_head = 2 * HEAD_DIM
    hps = 4

    def body(qn_ref, qp_ref, cc_ref, ck_ref, nc_ref, nk_ref, w_ref, g_ref, gate_ref, o_ref,
             lat_ref, spe_ref, p_ref, linv_ref, lat_t_ref):
        hp = pl.program_id(1)

        @pl.when(hp == 0)
        def _():
            lat_ref[0:past, :] = cc_ref[...].astype(BF16)
            lat_ref[past:rows, :] = _pad_rows(nc_ref[...], NEW_PAD)
            q_all = jnp.concatenate([qp_ref[:, h * LANES:(h + 1) * LANES] for h in range(HEADS)], axis=0)
            kp = ck_ref[...].astype(BF16)
            spe_ref[:, 0:past] = jnp.dot(q_all, jnp.concatenate([kp, kp], axis=0), preferred_element_type=F32)
            spe_ref[:, past:rows] = lax.dot_general(q_all, _pad_rows(nk_ref[...], NEW_PAD),
                                                    (((1,), (1,)), ((), ())), preferred_element_type=F32)

            lat_t_ref[...] = lat_ref[...].T

        def up_project(pair):
            col = pl.multiple_of(pair * 2 * per_head, 2 * per_head)
            wk = jnp.concatenate([w_ref[:, pl.ds(col, HEAD_DIM)], w_ref[:, pl.ds(col + per_head, HEAD_DIM)]],
                                 axis=1).astype(BF16)
            return lax.dot_general(wk, lat_t_ref[...], (((0,), (0,)), ((), ())), preferred_element_type=F32)

        q_pos = past + lax.broadcasted_iota(jnp.int32, (sd, rows), 0)
        k_pos = lax.broadcasted_iota(jnp.int32, (sd, rows), 1)
        allowed = (k_pos // CHUNK <= q_pos // CHUNK) & (k_pos < past + sd)
        kks = [up_project(hp * (hps // 2) + pr) for pr in range(hps // 2)]
        for g in range(hps):
            hs = slice(g * HEAD_DIM, (g + 1) * HEAD_DIM)
            kk = kks[g // 2][(g % 2) * HEAD_DIM:(g % 2 + 1) * HEAD_DIM, :]
            kn = (_rms(kk, axis=0) * g_ref[...]).astype(BF16)
            r0 = pl.multiple_of((hps * hp + g) * sd, sd)
            s = jnp.dot(qn_ref[:, hs], kn, preferred_element_type=F32)
            s = jnp.where(allowed, s + spe_ref[pl.ds(r0, sd), :], NEG_INF)
            p = jnp.exp2(s - jnp.max(s, axis=-1, keepdims=True))
            linv_ref[pl.ds(r0, sd), :] = 1.0 / jnp.sum(p, axis=-1, keepdims=True)
            p_ref[pl.ds(r0, sd), :] = p.astype(BF16)

        @pl.when(hp == HEADS // hps - 1)
        def _():
            ctx = jnp.dot(p_ref[...], lat_ref[...], preferred_element_type=F32)
            for h in range(HEADS):
                rs = slice(h * sd, (h + 1) * sd)
                wv = w_ref[:, h * per_head + HEAD_DIM:(h + 1) * per_head].astype(BF16)
                out = jnp.dot(ctx[rs, :].astype(BF16), wv, preferred_element_type=F32) * linv_ref[rs, :]
                hs = slice(h * HEAD_DIM, (h + 1) * HEAD_DIM)
                o_ref[:, hs] = (out * gate_ref[:, hs].astype(F32)).astype(BF16)

    return pl.pallas_call(
        body,
        out_shape=jax.ShapeDtypeStruct((ts, WIDTH), BF16),
        grid=(nb, HEADS // hps),
        in_specs=[
            pl.BlockSpec((sd, hps * HEAD_DIM), lambda b, h: (b, h)),
            pl.BlockSpec((sd, WIDTH), lambda b, h: (b, 0)),
            pl.BlockSpec((None, past, KV_LORA), lambda b, h: (layer * nb + b, 0, 0)),
            pl.BlockSpec((None, ROPE_DIM, past), lambda b, h: (layer * nb + b, 0, 0)),
            pl.BlockSpec((sd, KV_LORA), lambda b, h: (b, 0)),
            pl.BlockSpec((sd, LANES), lambda b, h: (b, 0)),
            pl.BlockSpec((None, KV_LORA, HEADS * per_head), lambda b, h: (layer, 0, 0)),
            pl.BlockSpec((HEAD_DIM, 1), lambda b, h: (0, 0)),
            pl.BlockSpec((sd, WIDTH), lambda b, h: (b, 1)),
        ],
        out_specs=pl.BlockSpec((sd, WIDTH), lambda b, h: (b, 0)),
        scratch_shapes=[pltpu.VMEM((rows, KV_LORA), BF16), pltpu.VMEM((hq, rows), F32),
                        pltpu.VMEM((hq, rows), BF16), pltpu.VMEM((hq, 1), F32),
                        pltpu.VMEM((KV_LORA, rows), BF16)],
        compiler_params=_cparams("parallel", "arbitrary"),
        name=name,
    )(q_nope, q_pe, cache_ckv, cache_kpe_t, new_ckv, new_kpe2, w_kvb, g_kn, gate)


def _out_proj_call(o_a, o_b, w_out, layer, x, name):
    t, d = x.shape
    tm, tn = _row_tile(t, 1024), 512

    def body(a_ref, b_ref, wa_ref, wb_ref, x_ref, y_ref):
        acc = jnp.dot(a_ref[...], wa_ref[...].astype(BF16), preferred_element_type=F32)
        acc = acc + jnp.dot(b_ref[...], wb_ref[...].astype(BF16), preferred_element_type=F32)
        y_ref[...] = x_ref[...] + acc

    return pl.pallas_call(
        body,
        out_shape=jax.ShapeDtypeStruct((t, d), F32),
        grid=(t // tm, d // tn),
        in_specs=[
            pl.BlockSpec((tm, WIDTH), lambda i, j: (i, 0)),
            pl.BlockSpec((tm, WIDTH), lambda i, j: (i, 0)),
            pl.BlockSpec((None, WIDTH, tn), lambda i, j: (layer, 0, j)),
            pl.BlockSpec((None, WIDTH, tn), lambda i, j: (layer, 1, j)),
            pl.BlockSpec((tm, tn), lambda i, j: (i, j)),
        ],
        out_specs=pl.BlockSpec((tm, tn), lambda i, j: (i, j)),
        compiler_params=_cparams("parallel", "parallel"),
        name=name,
    )(o_a, o_b, w_out, w_out, x)


def _rope_tables(pos):
    half = ROPE_DIM // 2
    inv_freq = 1.0 / (ROPE_THETA ** (jnp.arange(half, dtype=F32) / half))
    ang = pos.astype(F32)[:, None] * inv_freq[None, :]
    cos, sin = jnp.cos(ang), jnp.sin(ang)
    cos64 = jnp.concatenate([cos, cos], axis=-1)
    sin64 = jnp.concatenate([-sin, sin], axis=-1)
    return jnp.concatenate([cos64, cos64], axis=-1), jnp.concatenate([sin64, sin64], axis=-1)


def _prep_w_in_call(w_in_t, layer, name):
    _, n_in, d = w_in_t.shape
    tc = _row_tile(d, 256)
    src = {}
    o = 0
    for seg, size in (("q", WIDTH), ("k", WIDTH), ("v", WIDTH), ("f", HEADS), ("za", WIDTH),
                      ("cq", Q_LORA), ("ckv", KV_LORA), ("kpe", ROPE_DIM), ("zb", WIDTH)):
        src[seg] = (o, size)
        o += size
    assert o == n_in
    dst = (("q", COL_Q), ("k", COL_K), ("v", COL_V), ("za", COL_GATE), ("zb", COL_GATE + WIDTH), ("cq", COL_CQ),
           ("ckv", COL_CKV), ("kpe", COL_SMALL), ("f", COL_SMALL + LANES))

    def body(x_ref, o_ref):
        for seg, d0 in dst:
            s0, size = src[seg]
            o_ref[d0:d0 + size, :] = x_ref[s0:s0 + size, :].astype(BF16)
        pad0 = COL_SMALL + ROPE_DIM
        o_ref[pad0:COL_SMALL + LANES, :] = jnp.zeros((LANES - ROPE_DIM, tc), BF16)
        pad1 = COL_SMALL + LANES + HEADS
        o_ref[pad1:COL_SMALL + N_SMALL, :] = jnp.zeros((LANES - HEADS, tc), BF16)

    return pl.pallas_call(
        body,
        out_shape=jax.ShapeDtypeStruct((COL_SMALL + N_SMALL, d), BF16),
        grid=(d // tc,),
        in_specs=[pl.BlockSpec((None, n_in, tc), lambda i: (layer, 0, i))],
        out_specs=pl.BlockSpec((COL_SMALL + N_SMALL, tc), lambda i: (0, i)),
        compiler_params=_cparams("parallel"),
        name=name,
    )(w_in_t)


def _prep_w_qb_call(w_qb, layer, name):
    _, k, n = w_qb.shape
    per_head = NOPE_DIM + ROPE_DIM
    tr = _row_tile(k, 256)

    def body(x_ref, o_ref):
        for hh in range(HEADS):
            o_ref[:, hh * NOPE_DIM:(hh + 1) * NOPE_DIM] = x_ref[:, hh * per_head:hh * per_head + NOPE_DIM].astype(BF16)
        for p in range(HEADS // 2):
            pe = [x_ref[:, hh * per_head + NOPE_DIM:(hh + 1) * per_head] for hh in (2 * p, 2 * p + 1)]
            o_ref[:, WIDTH + p * LANES:WIDTH + (p + 1) * LANES] = jnp.concatenate(pe, axis=1).astype(BF16)

    return pl.pallas_call(
        body,
        out_shape=jax.ShapeDtypeStruct((k, n), BF16),
        grid=(k // tr,),
        in_specs=[pl.BlockSpec((None, tr, n), lambda i: (layer, i, 0))],
        out_specs=pl.BlockSpec((tr, n), lambda i: (i, 0)),
        compiler_params=_cparams("parallel"),
        name=name,
    )(w_qb)


def _layer_params(l, g_norm, b_f, g_q_fox, g_k_fox, g_cq, g_qn, g_qp, g_ckv, g_kp, g_kn):
    row = lambda a: a.reshape(1, -1).astype(F32)
    pad = lambda a: jnp.concatenate([row(a), jnp.zeros((1, LANES - a.shape[-1]), F32)], axis=1)
    return dict(
        g_norm=row(g_norm[l]), b_f_row=pad(b_f[l]), g_q=row(g_q_fox[l]), g_k=row(g_k_fox[l]), g_cq=row(g_cq[l]),
        g_qn=row(g_qn[l]), g_qp2=jnp.concatenate([row(g_qp[l]), row(g_qp[l])], axis=1),
        g_ckv=row(g_ckv[l]), g_kp=pad(g_kp[l]), g_kn=row(g_kn[l]),
    )


def _input_stage(x, w, w_qb, p, cos, sin, layer, prev, depth, nb, sl, prompt, tag):
    h = _rmsnorm_call(x, p["g_norm"], f"norm_{tag}")
    prev = prev or {}
    r = {}
    r["q"] = _proj_heads(h, w, COL_Q, p["g_q"], FOX_SCALE, nb, sl, prompt, f"proj_q_{tag}",
                         w_rows_are_outputs=True)
    k_out = _proj_kv_fox(h, w, COL_K, p["g_k"], layer, prev.get("k"), depth, nb, sl, prompt, f"proj_k_{tag}")
    v_out = _proj_kv_fox(h, w, COL_V, None, layer, prev.get("v"), depth, nb, sl, prompt, f"proj_v_{tag}")
    r["k"], r["v"] = k_out[0], v_out[0]
    if prompt:
        r["k_bf"], r["v_bf"] = k_out[1], v_out[1]
    r["gate"] = _proj_gate(h, w, f"proj_gate_{tag}")
    cq = _proj_cq(h, w, p["g_cq"], f"proj_cq_{tag}")
    r["ckv"], r["ckv_bf"] = _proj_ckv(h, w, p["g_ckv"], layer, prev.get("ckv"), depth, f"proj_ckv_{tag}")
    r["kpe"], r["kpe2"], r["logf"], r["cum"] = _proj_small(
        h, w, p["g_kp"], p["b_f_row"], cos, sin, layer, prev.get("kpe"), prev.get("logf"), depth, nb, sl, prompt,
        f"proj_small_{tag}")
    r["q_nope"] = _proj_heads(cq, w_qb, 0, p["g_qn"], MLA_SCALE, nb, sl, prompt, f"proj_qn_{tag}")
    r["q_pe"] = _proj_q_pe(cq, w_qb, p["g_qp2"], cos, sin, nb, sl, prompt, f"proj_qp_{tag}")
    return r


def kernel(x_prompt, x_sample, cache_fox_k, cache_fox_v, cache_fox_logf, cache_mla_ckv, cache_mla_kpe, g_norm,
           w_in, b_f, g_q_fox, g_k_fox, g_cq, w_qb, g_qn, g_qp, g_ckv, g_kp, w_kvb, g_kn, w_out):
    nb, sl, d = x_prompt.shape
    nbd, sd, _ = x_sample.shape
    depth = w_in.shape[0]
    past = cache_fox_k.shape[2]
    assert sl % CHUNK == 0 and past % CHUNK == 0 and sd <= CHUNK, "chunk-aligned streaming shapes only"

    cos_p, sin_p = _rope_tables(jnp.arange(sl, dtype=jnp.int32))
    cos_p, sin_p = jnp.tile(cos_p, (nb, 1)), jnp.tile(sin_p, (nb, 1))
    cos_s, sin_s = _rope_tables(past + jnp.arange(sd, dtype=jnp.int32))
    cos_s, sin_s = jnp.tile(cos_s, (nbd, 1)), jnp.tile(sin_s, (nbd, 1))

    ck2 = cache_fox_k.reshape(-1, HEAD_DIM)
    cv2 = cache_fox_v.reshape(-1, HEAD_DIM)
    logf_t = jnp.swapaxes(cache_fox_logf, 2, 3)
    w_in_t = jnp.swapaxes(w_in, 1, 2)
    cckv = cache_mla_ckv.reshape(depth * nbd, past, KV_LORA)
    ckpe = jnp.swapaxes(cache_mla_kpe, 2, 3).reshape(depth * nbd, ROPE_DIM, past)

    y_p = x_prompt.reshape(nb * sl, d)
    y_s = x_sample.reshape(nbd * sd, d)
    prev_p, prev_s = None, None
    for l in range(depth):
        p = _layer_params(l, g_norm, b_f, g_q_fox, g_k_fox, g_cq, g_qn, g_qp, g_ckv, g_kp, g_kn)
        w = _prep_w_in_call(w_in_t, l, f"prep_w_in_{l}")
        wq = _prep_w_qb_call(w_qb, l, f"prep_w_qb_{l}")
        rp = _input_stage(y_p, w, wq, p, cos_p, sin_p, l, prev_p, depth, nb, sl, True, f"p{l}")
        kn_p, v_p = _proj_kv_up(rp["ckv_bf"], w_kvb, l, p["g_kn"], nb, sl, f"kv_up_p{l}")
        o_a = _flash_call([rp["q"]], rp["k_bf"], rp["cum"].reshape(nb, sl, LANES), rp["v_bf"], rp["gate"], 0,
                          nb, sl, True, f"fox_attn_p{l}")
        o_b = _flash_call([rp["q_nope"], rp["q_pe"]], kn_p, rp["kpe2"].reshape(nb, sl, LANES), v_p, rp["gate"],
                          HEADS, nb, sl, False, f"mla_attn_p{l}")
        y_p = _out_proj_call(o_a, o_b, w_out, l, y_p, f"out_proj_p{l}")
        prev_p = rp
        rs = _input_stage(y_s, w, wq, p, cos_s, sin_s, l, prev_s, depth, nbd, sd, False, f"s{l}")
        o_a = _fox_decode_call(rs["q"], ck2, cv2, rs["k"], rs["v"], l, logf_t, rs["cum"], rs["gate"], nbd, sd,
                               past, f"fox_attn_s{l}")
        o_b = _mla_decode_call(rs["q_nope"], rs["q_pe"], cckv, ckpe, rs["ckv_bf"], rs["kpe2"], l, w_kvb,
                               p["g_kn"].reshape(HEAD_DIM, 1), rs["gate"], nbd, sd, past, f"mla_attn_s{l}")
        y_s = _out_proj_call(o_a, o_b, w_out, l, y_s, f"out_proj_s{l}")
        prev_s = rs

    hk = (HEADS, HEAD_DIM)
    return (y_p.reshape(nb, sl, d), y_s.reshape(nbd, sd, d),
            prev_p["k"].reshape((depth, nb, sl) + hk), prev_p["v"].reshape((depth, nb, sl) + hk),
            prev_p["logf"].reshape(depth, nb, sl, HEADS), prev_p["ckv"].reshape(depth, nb, sl, KV_LORA),
            prev_p["kpe"].reshape(depth, nb, sl, ROPE_DIM),
            prev_s["k"].reshape((depth, nbd, sd) + hk), prev_s["v"].reshape((depth, nbd, sd) + hk),
            prev_s["logf"].reshape(depth, nbd, sd, HEADS), prev_s["ckv"].reshape(depth, nbd, sd, KV_LORA),
            prev_s["kpe"].reshape(depth, nbd, sd, ROPE_DIM))
```

```python
import math

import jax
import jax.numpy as jnp
from jax import lax
from jax.experimental import pallas as pl
from jax.experimental.pallas import tpu as pltpu

F32 = jnp.float32
BF16 = jnp.bfloat16

CHUNK = 64
HEADS = 16
HEAD_DIM = 128
WIDTH = HEADS * HEAD_DIM
NOPE_DIM = 128
ROPE_DIM = 64
Q_LORA = 1024
KV_LORA = 512
ROPE_THETA = 10000.0
EPS = 1e-6
LOG2E = math.log2(math.e)
FOX_SCALE = LOG2E / math.sqrt(HEAD_DIM)
MLA_SCALE = LOG2E / math.sqrt(NOPE_DIM + ROPE_DIM)
LANES = 128
SUBLANES = 8
NEW_PAD = 128
VMEM_LIMIT_BYTES = 52 * 1024 * 1024
NEG_INF = float("-inf")

COL_Q, COL_K, COL_V, COL_GATE = 0, WIDTH, 2 * WIDTH, 3 * WIDTH
COL_CQ = 5 * WIDTH
COL_CKV = COL_CQ + Q_LORA
COL_SMALL = COL_CKV + KV_LORA
N_SMALL = 2 * LANES


def _cparams(*sem):
    return pltpu.CompilerParams(dimension_semantics=sem, vmem_limit_bytes=VMEM_LIMIT_BYTES)


def _row_tile(t, cap):
    if t <= cap:
        return t
    for c in range(cap, 15, -16):
        if t % c == 0:
            return c
    return t


def _rms(x, n=None, axis=-1):
    n = x.shape[axis] if n is None else n
    ms = jnp.sum(x * x, axis=axis, keepdims=True) * (1.0 / n)
    return x * lax.rsqrt(ms + EPS)


def _silu(x):
    return x * (1.0 / (1.0 + jnp.exp(-x)))


def _log_sigmoid(x):
    return jnp.minimum(x, 0.0) - jnp.log1p(jnp.exp(-jnp.abs(x)))


def _scan(x, axis, length, pos):
    sh = 1
    while sh < length:
        x = x + jnp.where(pos >= sh, pltpu.roll(x, sh, axis), 0.0)
        sh *= 2
    return x


def _rmsnorm_call(x, g, name):
    t, d = x.shape
    tm = _row_tile(t, 256)

    def body(x_ref, g_ref, o_ref):
        o_ref[...] = (_rms(x_ref[...]) * g_ref[...]).astype(BF16)

    return pl.pallas_call(
        body,
        out_shape=jax.ShapeDtypeStruct((t, d), BF16),
        grid=(t // tm,),
        in_specs=[pl.BlockSpec((tm, d), lambda i: (i, 0)), pl.BlockSpec((1, d), lambda i: (0, 0))],
        out_specs=pl.BlockSpec((tm, d), lambda i: (i, 0)),
        compiler_params=_cparams("parallel"),
        name=name,
    )(x, g)


def _proj_call(x, w, col0, n, tm, tn, extras, extra_specs, out_shapes, out_specs, epilogue, name,
               alias_inputs=(), scratch_shapes=(), layer=None, w_rows_are_outputs=False, row_chunks=1,
               whole_tile_epilogue=False):
    t, k = x.shape
    assert col0 % tn == 0 and n % tn == 0 and t % tm == 0
    cb0 = col0 // tn
    n_ex, n_al, n_out = len(extras), len(alias_inputs), len(out_shapes)
    contract = (((1,), (1,)), ((), ())) if w_rows_are_outputs else (((1,), (0,)), ((), ()))

    assert tm % (16 * row_chunks) == 0
    nr = tm // row_chunks

    def body(x_ref, w_ref, *refs):
        ex = refs[:n_ex]
        outs = refs[n_ex + n_al:n_ex + n_al + n_out]
        scr = refs[n_ex + n_al + n_out:]
        wv = w_ref[...].astype(BF16)
        accs = []
        for c in range(row_chunks):
            rows = slice(c * nr, (c + 1) * nr)
            acc = lax.dot_general(x_ref[rows, :], wv, contract, preferred_element_type=F32)
            if whole_tile_epilogue:
                accs.append(acc)
            else:
                epilogue(acc, rows, x_ref, ex, outs, scr)
        if whole_tile_epilogue:
            epilogue(jnp.concatenate(accs, axis=0), slice(0, tm), x_ref, ex, outs, scr)

    if w_rows_are_outputs:
        assert layer is None
        w_spec = pl.BlockSpec((tn, k), lambda i, j: (cb0 + j, 0))
    elif layer is None:
        w_spec = pl.BlockSpec((k, tn), lambda i, j: (0, cb0 + j))
    else:
        w_spec = pl.BlockSpec((None, k, tn), lambda i, j: (layer, 0, cb0 + j))
    in_specs = [pl.BlockSpec((tm, k), lambda i, j: (i, 0)), w_spec]
    in_specs += list(extra_specs)
    in_specs += [pl.BlockSpec(memory_space=pl.ANY) for _ in alias_inputs]
    aliases = {2 + n_ex + a: out_idx for a, (_, out_idx) in enumerate(alias_inputs)}
    return pl.pallas_call(
        body,
        out_shape=out_shapes,
        grid=(t // tm, n // tn),
        in_specs=in_specs,
        out_specs=out_specs,
        scratch_shapes=list(scratch_shapes),
        input_output_aliases=aliases,
        compiler_params=_cparams("arbitrary", "arbitrary"),
        name=name,
    )(x, w, *extras, *[a for a, _ in alias_inputs])


def _chunks(tm):
    return 2 if tm % 512 == 0 else 1


def _const_spec(shape):
    nd = len(shape)
    return pl.BlockSpec(shape, lambda i, j: (0,) * nd)


def _head_out(nb, sl, tm, nh_t, head_major):
    if head_major:
        nsb = sl // tm
        shape = jax.ShapeDtypeStruct((nb, HEADS, sl, HEAD_DIM), BF16)
        spec = pl.BlockSpec((None, nh_t, tm, HEAD_DIM), lambda i, j: (i // nsb, j, i % nsb, 0))
    else:
        shape = jax.ShapeDtypeStruct((nb * sl, WIDTH), BF16)
        spec = pl.BlockSpec((tm, nh_t * HEAD_DIM), lambda i, j: (i, j))
    return shape, spec


def _store_head(o_ref, hh, rows, val, head_major):
    if head_major:
        o_ref[hh, rows, :] = val
    else:
        o_ref[rows, hh * HEAD_DIM:(hh + 1) * HEAD_DIM] = val


def _proj_heads(h, w, col0, g, scale, nb, sl, head_major, name, w_rows_are_outputs=False):
    t = h.shape[0]
    tm, tn = _row_tile(min(t, sl) if head_major else t, 1024), 1024
    nh_t = tn // HEAD_DIM
    shape, spec = _head_out(nb, sl, tm, nh_t, head_major)

    def epi(acc, rows, x_ref, ex, outs, scr):
        g_ref, = ex
        for hh in range(nh_t):
            v = _rms(acc[:, hh * HEAD_DIM:(hh + 1) * HEAD_DIM]) * g_ref[...] * scale
            _store_head(outs[0], hh, rows, v.astype(BF16), head_major)

    return _proj_call(h, w, col0, WIDTH, tm, tn, [g], [_const_spec((1, HEAD_DIM))], [shape], [spec], epi, name,
                      w_rows_are_outputs=w_rows_are_outputs, row_chunks=_chunks(tm))[0]


def _proj_kv_fox(h, w, col0, g, layer, prev, depth, nb, sl, head_major, name):
    t = h.shape[0]
    tm = _row_tile(min(t, sl) if head_major else t, 1024)
    tn = 512 if tm > 256 else 1024
    nh_t = tn // HEAD_DIM
    f_shape = jax.ShapeDtypeStruct((depth, t * HEADS, HEAD_DIM), F32)
    f_spec = pl.BlockSpec((None, tm * HEADS, HEAD_DIM), lambda i, j: (layer, i, 0))
    shapes, specs = [f_shape], [f_spec]
    if head_major:
        b_shape, b_spec = _head_out(nb, sl, tm, nh_t, True)
        shapes.append(b_shape)
        specs.append(b_spec)
    extras = [] if g is None else [g]
    especs = [] if g is None else [_const_spec((1, HEAD_DIM))]

    def epi(acc, rows, x_ref, ex, outs, scr):
        j = pl.program_id(1)
        for hh in range(nh_t):
            v = acc[:, hh * HEAD_DIM:(hh + 1) * HEAD_DIM]
            if g is not None:
                v = _rms(v) * ex[0][...]
            nr = rows.stop - rows.start
            outs[0][pl.ds(rows.start * HEADS + j * nh_t + hh, nr, stride=HEADS), :] = v
            if head_major:
                outs[1][hh, rows, :] = v.astype(BF16)

    alias = [] if prev is None else [(prev, 0)]
    return _proj_call(h, w, col0, WIDTH, tm, tn, extras, especs, shapes, specs, epi, name, alias_inputs=alias,
                      w_rows_are_outputs=True, row_chunks=_chunks(tm))


def _proj_gate(h, w, name):
    t = h.shape[0]
    tm, tn = _row_tile(t, 1024), 1024

    def epi(acc, rows, x_ref, ex, outs, scr):
        outs[0][rows, :] = _silu(acc).astype(BF16)

    return _proj_call(h, w, COL_GATE, 2 * WIDTH, tm, tn, [], [], [jax.ShapeDtypeStruct((t, 2 * WIDTH), BF16)],
                      [pl.BlockSpec((tm, tn), lambda i, j: (i, j))], epi, name, w_rows_are_outputs=True,
                      row_chunks=_chunks(tm))[0]


def _proj_cq(h, w, g, name):
    t = h.shape[0]
    tm, tn = _row_tile(t, 1024), Q_LORA

    def epi(acc, rows, x_ref, ex, outs, scr):
        outs[0][rows, :] = (_rms(acc) * ex[0][...]).astype(BF16)

    return _proj_call(h, w, COL_CQ, Q_LORA, tm, tn, [g], [_const_spec((1, Q_LORA))],
                      [jax.ShapeDtypeStruct((t, Q_LORA), BF16)], [pl.BlockSpec((tm, tn), lambda i, j: (i, 0))],
                      epi, name, w_rows_are_outputs=True, row_chunks=_chunks(tm))[0]


def _proj_ckv(h, w, g, layer, prev, depth, name):
    t = h.shape[0]
    tm, tn = _row_tile(t, 1024), KV_LORA

    def epi(acc, rows, x_ref, ex, outs, scr):
        v = _rms(acc) * ex[0][...]
        outs[0][rows, :] = v
        outs[1][rows, :] = v.astype(BF16)

    shapes = [jax.ShapeDtypeStruct((depth, t, KV_LORA), F32), jax.ShapeDtypeStruct((t, KV_LORA), BF16)]
    specs = [pl.BlockSpec((None, tm, tn), lambda i, j: (layer, i, 0)), pl.BlockSpec((tm, tn), lambda i, j: (i, 0))]
    alias = [] if prev is None else [(prev, 0)]
    return _proj_call(h, w, COL_CKV, KV_LORA, tm, tn, [g], [_const_spec((1, KV_LORA))], shapes, specs, epi, name,
                      alias_inputs=alias, w_rows_are_outputs=True, row_chunks=_chunks(tm))


def _swap_halves(x):
    lane = lax.broadcasted_iota(jnp.int32, x.shape, 1)
    return jnp.where((lane % ROPE_DIM) < ROPE_DIM // 2, pltpu.roll(x, LANES - ROPE_DIM // 2, 1),
                     pltpu.roll(x, ROPE_DIM // 2, 1))


def _proj_small(h, w, g_kp, b_f_row, cos, sin, layer, prev_kpe, prev_logf, depth, nb, sl, prompt, name):
    t = h.shape[0]
    tm = _row_tile(min(t, sl), 1024) if prompt else t
    nsb = sl // tm if prompt else 1
    seqs_per_blk = 1 if prompt else tm // sl

    def epi(acc, rows, x_ref, ex, outs, scr):
        g_ref, bfr_ref, cos_ref, sin_ref = ex
        kpe_ref, kpe2_ref, logf_ref, cum_ref = outs
        i = pl.program_id(0)
        kp = _rms(acc[:, :LANES], ROPE_DIM) * g_ref[...]
        kp = kp * cos_ref[...] + _swap_halves(kp) * sin_ref[...]
        kpe_ref[...] = kp[:, :ROPE_DIM]
        kpe2_ref[...] = (kp + pltpu.roll(kp, ROPE_DIM, 1)).astype(BF16)
        lf = _log_sigmoid(acc[:, LANES:2 * LANES] + bfr_ref[...])
        logf_ref[...] = lf[:, :HEADS]
        if prompt:
            carry_ref, = scr
            lane = lax.broadcasted_iota(jnp.int32, lf.shape, 1)
            row = lax.broadcasted_iota(jnp.int32, lf.shape, 0)
            c = _scan(jnp.where(lane < HEADS, lf, 0.0), 0, tm, row)
            if nsb > 1:
                @pl.when(i % nsb == 0)
                def _():
                    carry_ref[...] = jnp.zeros_like(carry_ref)
                c = c + carry_ref[0:1, :]
                carry_ref[...] = jnp.broadcast_to(c[tm - 1:tm, :], carry_ref.shape)
            neg = c * (-LOG2E)
            hi = neg.astype(BF16).astype(F32)
            mid = (neg - hi).astype(BF16).astype(F32)
            lo = neg - hi - mid
            cum_ref[...] = (hi + pltpu.roll(mid, HEADS, 1) + pltpu.roll(lo, 2 * HEADS, 1)).astype(BF16)
        else:
            c = lf.T[:HEADS, :]
            pos = lax.broadcasted_iota(jnp.int32, c.shape, 1) % sl
            c = _scan(c, 1, sl, pos) * (-LOG2E)
            cum_ref[...] = jnp.zeros(cum_ref.shape, F32)
            for s in range(seqs_per_blk):
                cum_ref[s, :, 0:sl] = c[:, s * sl:(s + 1) * sl]

    if prompt:
        cum_shape = jax.ShapeDtypeStruct((t, LANES), BF16)
        cum_spec = pl.BlockSpec((tm, LANES), lambda i, j: (i, 0))
        scratch = [pltpu.VMEM((SUBLANES, LANES), F32)]
    else:
        assert sl <= LANES
        cum_shape = jax.ShapeDtypeStruct((nb, HEADS, LANES), F32)
        cum_spec = pl.BlockSpec((seqs_per_blk, HEADS, LANES), lambda i, j: (i, 0, 0))
        scratch = []
    shapes = [jax.ShapeDtypeStruct((depth, t, ROPE_DIM), F32), jax.ShapeDtypeStruct((t, LANES), BF16),
              jax.ShapeDtypeStruct((depth, t, HEADS), F32), cum_shape]
    specs = [pl.BlockSpec((None, tm, ROPE_DIM), lambda i, j: (layer, i, 0)),
             pl.BlockSpec((tm, LANES), lambda i, j: (i, 0)),
             pl.BlockSpec((None, tm, HEADS), lambda i, j: (layer, i, 0)), cum_spec]
    extras = [g_kp, b_f_row, cos, sin]
    especs = [_const_spec((1, LANES)), _const_spec((1, LANES)),
              pl.BlockSpec((tm, LANES), lambda i, j: (i, 0)), pl.BlockSpec((tm, LANES), lambda i, j: (i, 0))]
    alias = []
    if prev_kpe is not None:
        alias = [(prev_kpe, 0), (prev_logf, 2)]
    return _proj_call(h, w, COL_SMALL, N_SMALL, tm, N_SMALL, extras, especs, shapes, specs, epi, name,
                      alias_inputs=alias, scratch_shapes=scratch, w_rows_are_outputs=True,
                      row_chunks=_chunks(tm), whole_tile_epilogue=True)


def _proj_q_pe(cq, w, g2, cos, sin, nb, sl, head_major, name):
    t = cq.shape[0]
    tm, tn = _row_tile(min(t, sl) if head_major else t, 512), HEADS * ROPE_DIM
    shape, spec = _head_out(nb, sl, tm, HEADS, head_major)

    def epi(acc, rows, x_ref, ex, outs, scr):
        g_ref, cos_ref, sin_ref = ex
        lane = lax.broadcasted_iota(jnp.int32, (acc.shape[0], LANES), 1)
        lo = lane < ROPE_DIM
        cos_t, sin_t = cos_ref[rows, :], sin_ref[rows, :]
        for p in range(HEADS // 2):
            x = acc[:, p * LANES:(p + 1) * LANES]
            x2 = x * x
            ms_lo = jnp.sum(jnp.where(lo, x2, 0.0), axis=-1, keepdims=True)
            ms_hi = jnp.sum(jnp.where(lo, 0.0, x2), axis=-1, keepdims=True)
            ms = jnp.where(lo, ms_lo, ms_hi) * (1.0 / ROPE_DIM)
            y = x * lax.rsqrt(ms + EPS) * g_ref[...]
            y = (y * cos_t + _swap_halves(y) * sin_t) * MLA_SCALE
            _store_head(outs[0], 2 * p, rows, jnp.where(lo, y, 0.0).astype(BF16), head_major)
            _store_head(outs[0], 2 * p + 1, rows, jnp.where(lo, 0.0, y).astype(BF16), head_major)

    especs = [_const_spec((1, LANES)), pl.BlockSpec((tm, LANES), lambda i, j: (i, 0)),
              pl.BlockSpec((tm, LANES), lambda i, j: (i, 0))]
    return _proj_call(cq, w, WIDTH, tn, tm, tn, [g2, cos, sin], especs, [shape], [spec], epi, name,
                      row_chunks=_chunks(tm))[0]


def _proj_kv_up(ckv, w, layer, g, nb, sl, name):
    t = ckv.shape[0]
    tm, tn = _row_tile(min(t, sl), 1024), 1024
    nh_t = tn // (2 * HEAD_DIM)
    nsb = sl // tm
    shape = jax.ShapeDtypeStruct((nb, HEADS, sl, HEAD_DIM), BF16)
    spec = pl.BlockSpec((None, nh_t, tm, HEAD_DIM), lambda i, j: (i // nsb, j, i % nsb, 0))

    def epi(acc, rows, x_ref, ex, outs, scr):
        for hh in range(nh_t):
            base = hh * 2 * HEAD_DIM
            outs[0][hh, rows, :] = (_rms(acc[:, base:base + HEAD_DIM]) * ex[0][...]).astype(BF16)
            outs[1][hh, rows, :] = acc[:, base + HEAD_DIM:base + 2 * HEAD_DIM].astype(BF16)

    return _proj_call(ckv, w, 0, 2 * WIDTH, tm, tn, [g], [_const_spec((1, HEAD_DIM))], [shape, shape],
                      [spec, spec], epi, name, layer=layer, row_chunks=_chunks(tm))


FLASH_HEADS_PER_STEP = 2


def _flash_call(q_parts, k_head, k_shared, v, gate, gate_col0, nb, sl, fox, name):
    tq = _row_tile(sl, 256)
    nqb = sl // tq
    t = nb * sl
    n_q = len(q_parts)

    hps = FLASH_HEADS_PER_STEP
    chains = [(hh, qi) for hh in range(hps) for qi in range(nqb)]

    def body(*refs):
        q_refs = refs[:n_q]
        kh_ref, ks_ref, v_ref, g_ref, o_ref = refs[n_q:]
        hp = pl.program_id(1)
        key = lax.broadcasted_iota(jnp.int32, (tq, tq), 0)
        qry = lax.broadcasted_iota(jnp.int32, (tq, tq), 1)
        allowed = (key <= qry) if fox else (key // CHUNK <= qry // CHUNK)
        if fox:
            lane = lax.broadcasted_iota(jnp.int32, (tq, LANES), 1)
            selectors = [jnp.where((lane % HEADS == hp * hps + hh) & (lane < 3 * HEADS), 1.0, 0.0).astype(BF16)
                         for hh in range(hps)]

        def q_block(hh, qi):
            rows = slice(qi * tq, (qi + 1) * tq)
            parts = [r[hh, rows, :] for r in q_refs] + ([selectors[hh]] if fox else [])
            return jnp.concatenate(parts, axis=-1)

        def k_block(hh, kb):
            rows = slice(kb * tq, (kb + 1) * tq)
            return jnp.concatenate([kh_ref[hh, rows, :], ks_ref[rows, :]], axis=-1)

        def score(rnd, hh, qi):
            s = lax.dot_general(k_block(hh, qi - rnd), q_block(hh, qi), (((1,), (1,)), ((), ())),
                                preferred_element_type=F32)
            return jnp.where(allowed, s, NEG_INF) if rnd == 0 else s

        state = {c: (jnp.full((1, tq), NEG_INF, F32), jnp.zeros((1, tq), F32), jnp.zeros((HEAD_DIM, tq), F32))
                 for c in chains}
        s_cur = {c: score(0, *c) for c in chains}
        for rnd in range(nqb):
            s_next = {}
            for hh, qi in chains:
                if qi < rnd:
                    continue
                if qi >= rnd + 1 and rnd + 1 < nqb:
                    s_next[hh, qi] = score(rnd + 1, hh, qi)
                m, l, acc = state[hh, qi]
                s = s_cur[hh, qi]
                m_new = jnp.maximum(m, jnp.max(s, axis=0, keepdims=True))
                alpha = jnp.exp2(m - m_new)
                p = jnp.exp2(s - m_new)
                l = alpha * l + jnp.sum(p, axis=0, keepdims=True)
                kb = qi - rnd
                pv = lax.dot_general(v_ref[hh, kb * tq:(kb + 1) * tq, :], p.astype(BF16),
                                     (((0,), (0,)), ((), ())), preferred_element_type=F32)
                state[hh, qi] = (m_new, l, alpha * acc + pv)
            s_cur = s_next
        for hh, qi in chains:
            m, l, acc = state[hh, qi]
            rows = slice(qi * tq, (qi + 1) * tq)
            cols = slice(hh * HEAD_DIM, (hh + 1) * HEAD_DIM)
            o_ref[rows, cols] = ((acc / l).T * g_ref[rows, cols].astype(F32)).astype(BF16)

    head_spec = pl.BlockSpec((None, hps, sl, HEAD_DIM), lambda b, h: (b, h, 0, 0))
    gate_cb0 = gate_col0 // hps
    in_specs = [head_spec for _ in q_parts] + [
        head_spec, pl.BlockSpec((None, sl, LANES), lambda b, h: (b, 0, 0)), head_spec,
        pl.BlockSpec((None, sl, hps * HEAD_DIM), lambda b, h: (b, 0, gate_cb0 + h))]
    return pl.pallas_call(
        body,
        out_shape=jax.ShapeDtypeStruct((nb, sl, WIDTH), BF16),
        grid=(nb, HEADS // hps),
        in_specs=in_specs,
        out_specs=pl.BlockSpec((None, sl, hps * HEAD_DIM), lambda b, h: (b, 0, h)),
        compiler_params=_cparams("parallel", "parallel"),
        name=name,
    )(*q_parts, k_head, k_shared, v, gate.reshape(nb, sl, 2 * WIDTH)).reshape(t, WIDTH)


def _pad_rows(x, rows):
    return jnp.concatenate([x, jnp.zeros((rows - x.shape[0], x.shape[1]), x.dtype)], axis=0)


def _fox_decode_call(q, cache_k, cache_v, new_k, new_v, layer, logf_t, cum_new, gate, nb, sd, past, name):
    tk = _row_tile(past, 512)
    nkc = past // tk
    ts = nb * sd
    rows = HEADS * sd

    def body(q_ref, ck_ref, cv_ref, nk_ref, nv_ref, lf_ref, cn_ref, g_ref, o_ref, cum_ref, m_ref, l_ref, acc_ref):
        kc = pl.program_id(1)

        @pl.when(kc == 0)
        def _():
            c = lf_ref[...]
            cum_ref[...] = _scan(c, 1, past, lax.broadcasted_iota(jnp.int32, c.shape, 1)) * (-LOG2E)
            m_ref[...] = jnp.full(m_ref.shape, NEG_INF, F32)
            l_ref[...] = jnp.zeros(l_ref.shape, F32)
            acc_ref[...] = jnp.zeros(acc_ref.shape, F32)

        def attend(keys, values, bias, mask):
            s = []
            for hh in range(HEADS):
                qh = q_ref[:, hh * HEAD_DIM:(hh + 1) * HEAD_DIM]
                sh = lax.dot_general(qh, keys(hh), (((1,), (1,)), ((), ())), preferred_element_type=F32)
                sh = sh + bias(hh)
                s.append(sh if mask is None else jnp.where(mask, sh, NEG_INF))
            s = jnp.concatenate(s, axis=0)
            m_old = m_ref[...]
            m_new = jnp.maximum(m_old, jnp.max(s, axis=-1, keepdims=True))
            alpha = jnp.exp2(m_old - m_new)
            p = jnp.exp2(s - m_new)
            l_ref[...] = alpha * l_ref[...] + jnp.sum(p, axis=-1, keepdims=True)
            m_ref[...] = m_new
            pb = p.astype(BF16)
            pv = [jnp.dot(pb[hh * sd:(hh + 1) * sd, :], values(hh), preferred_element_type=F32)
                  for hh in range(HEADS)]
            acc_ref[...] = alpha * acc_ref[...] + jnp.concatenate(pv, axis=0)

        def by_head(ref, n):
            return jnp.swapaxes(ref[...].astype(BF16).reshape(n, HEADS, HEAD_DIM), 0, 1)

        start = pl.multiple_of(kc * tk, tk)
        kk, vv = by_head(ck_ref, tk), by_head(cv_ref, tk)
        attend(lambda hh: kk[hh], lambda hh: vv[hh], lambda hh: cum_ref[hh:hh + 1, pl.ds(start, tk)], None)

        @pl.when(kc == nkc - 1)
        def _():
            row = lax.broadcasted_iota(jnp.int32, (sd, NEW_PAD), 0)
            col = lax.broadcasted_iota(jnp.int32, (sd, NEW_PAD), 1)
            nk, nv = by_head(nk_ref, sd), by_head(nv_ref, sd)
            attend(lambda hh: _pad_rows(nk[hh], NEW_PAD), lambda hh: _pad_rows(nv[hh], NEW_PAD),
                   lambda hh: cum_ref[hh:hh + 1, past - 1:past] + cn_ref[hh:hh + 1, :], col <= row)
            out = acc_ref[...] / l_ref[...]
            for hh in range(HEADS):
                gh = g_ref[:, hh * HEAD_DIM:(hh + 1) * HEAD_DIM].astype(F32)
                o_ref[:, hh * HEAD_DIM:(hh + 1) * HEAD_DIM] = (out[hh * sd:(hh + 1) * sd, :] * gh).astype(BF16)

    cache_spec = pl.BlockSpec((tk * HEADS, HEAD_DIM), lambda b, kc: ((layer * nb + b) * nkc + kc, 0))
    new_spec = pl.BlockSpec((None, sd * HEADS, HEAD_DIM), lambda b, kc: (layer, b, 0))
    return pl.pallas_call(
        body,
        out_shape=jax.ShapeDtypeStruct((ts, WIDTH), BF16),
        grid=(nb, nkc),
        in_specs=[
            pl.BlockSpec((sd, WIDTH), lambda b, kc: (b, 0)),
            cache_spec, cache_spec, new_spec, new_spec,
            pl.BlockSpec((None, None, HEADS, past), lambda b, kc: (layer, b, 0, 0)),
            pl.BlockSpec((None, HEADS, NEW_PAD), lambda b, kc: (b, 0, 0)),
            pl.BlockSpec((sd, WIDTH), lambda b, kc: (b, 0)),
        ],
        out_specs=pl.BlockSpec((sd, WIDTH), lambda b, kc: (b, 0)),
        scratch_shapes=[pltpu.VMEM((HEADS, past), F32), pltpu.VMEM((rows, 1), F32),
                        pltpu.VMEM((rows, 1), F32), pltpu.VMEM((rows, HEAD_DIM), F32)],
        compiler_params=_cparams("parallel", "arbitrary"),
        name=name,
    )(q, cache_k, cache_v, new_k, new_v, logf_t, cum_new, gate)


def _mla_decode_call(q_nope, q_pe, cache_ckv, cache_kpe_t, new_ckv, new_kpe2, layer, w_kvb, g_kn, gate,
                     nb, sd, past, name):
    ts = nb * sd
    rows = past + NEW_PAD
    hq = HEADS * sd
    per_head = 2 * HEAD_DIM
    hps = 4

    def body(qn_ref, qp_ref, cc_ref, ck_ref, nc_ref, nk_ref, w_ref, g_ref, gate_ref, o_ref,
             lat_ref, spe_ref, p_ref, linv_ref, lat_t_ref):
        hp = pl.program_id(1)

        @pl.when(hp == 0)
        def _():
            lat_ref[0:past, :] = cc_ref[...].astype(BF16)
            lat_ref[past:rows, :] = _pad_rows(nc_ref[...], NEW_PAD)
            q_all = jnp.concatenate([qp_ref[:, h * LANES:(h + 1) * LANES] for h in range(HEADS)], axis=0)
            kp = ck_ref[...].astype(BF16)
            spe_ref[:, 0:past] = jnp.dot(q_all, jnp.concatenate([kp, kp], axis=0), preferred_element_type=F32)
            spe_ref[:, past:rows] = lax.dot_general(q_all, _pad_rows(nk_ref[...], NEW_PAD),
                                                    (((1,), (1,)), ((), ())), preferred_element_type=F32)

            lat_t_ref[...] = lat_ref[...].T

        def up_project(pair):
            col = pl.multiple_of(pair * 2 * per_head, 2 * per_head)
            wk = jnp.concatenate([w_ref[:, pl.ds(col, HEAD_DIM)], w_ref[:, pl.ds(col + per_head, HEAD_DIM)]],
                                 axis=1).astype(BF16)
            return lax.dot_general(wk, lat_t_ref[...], (((0,), (0,)), ((), ())), preferred_element_type=F32)

        q_pos = past + lax.broadcasted_iota(jnp.int32, (sd, rows), 0)
        k_pos = lax.broadcasted_iota(jnp.int32, (sd, rows), 1)
        allowed = (k_pos // CHUNK <= q_pos // CHUNK) & (k_pos < past + sd)
        kks = [up_project(hp * (hps // 2) + pr) for pr in range(hps // 2)]
        for g in range(hps):
            hs = slice(g * HEAD_DIM, (g + 1) * HEAD_DIM)
            kk = kks[g // 2][(g % 2) * HEAD_DIM:(g % 2 + 1) * HEAD_DIM, :]
            kn = (_rms(kk, axis=0) * g_ref[...]).astype(BF16)
            r0 = pl.multiple_of((hps * hp + g) * sd, sd)
            s = jnp.dot(qn_ref[:, hs], kn, preferred_element_type=F32)
            s = jnp.where(allowed, s + spe_ref[pl.ds(r0, sd), :], NEG_INF)
            p = jnp.exp2(s - jnp.max(s, axis=-1, keepdims=True))
            linv_ref[pl.ds(r0, sd), :] = 1.0 / jnp.sum(p, axis=-1, keepdims=True)
            p_ref[pl.ds(r0, sd), :] = p.astype(BF16)

        @pl.when(hp == HEADS // hps - 1)
        def _():
            ctx = jnp.dot(p_ref[...], lat_ref[...], preferred_element_type=F32)
            for h in range(HEADS):
                rs = slice(h * sd, (h + 1) * sd)
                wv = w_ref[:, h * per_head + HEAD_DIM:(h + 1) * per_head].astype(BF16)
                out = jnp.dot(ctx[rs, :].astype(BF16), wv, preferred_element_type=F32) * linv_ref[rs, :]
                hs = slice(h * HEAD_DIM, (h + 1) * HEAD_DIM)
                o_ref[:, hs] = (out * gate_ref[:, hs].astype(F32)).astype(BF16)

    return pl.pallas_call(
        body,
        out_shape=jax.ShapeDtypeStruct((ts, WIDTH), BF16),
        grid=(nb, HEADS // hps),
        in_specs=[
            pl.BlockSpec((sd, hps * HEAD_DIM), lambda b, h: (b, h)),
            pl.BlockSpec((sd, WIDTH), lambda b, h: (b, 0)),
            pl.BlockSpec((None, past, KV_LORA), lambda b, h: (layer * nb + b, 0, 0)),
            pl.BlockSpec((None, ROPE_DIM, past), lambda b, h: (layer * nb + b, 0, 0)),
            pl.BlockSpec((sd, KV_LORA), lambda b, h: (b, 0)),
            pl.BlockSpec((sd, LANES), lambda b, h: (b, 0)),
            pl.BlockSpec((None, KV_LORA, HEADS * per_head), lambda b, h: (layer, 0, 0)),
            pl.BlockSpec((HEAD_DIM, 1), lambda b, h: (0, 0)),
            pl.BlockSpec((sd, WIDTH), lambda b, h: (b, 1)),
        ],
        out_specs=pl.BlockSpec((sd, WIDTH), lambda b, h: (b, 0)),
        scratch_shapes=[pltpu.VMEM((rows, KV_LORA), BF16), pltpu.VMEM((hq, rows), F32),
                        pltpu.VMEM((hq, rows), BF16), pltpu.VMEM((hq, 1), F32),
                        pltpu.VMEM((KV_LORA, rows), BF16)],
        compiler_params=_cparams("parallel", "arbitrary"),
        name=name,
    )(q_nope, q_pe, cache_ckv, cache_kpe_t, new_ckv, new_kpe2, w_kvb, g_kn, gate)


def _out_proj_call(o_a, o_b, w_out, layer, x, name):
    t, d = x.shape
    tm, tn = _row_tile(t, 1024), 512

    def body(a_ref, b_ref, wa_ref, wb_ref, x_ref, y_ref):
        acc = jnp.dot(a_ref[...], wa_ref[...].astype(BF16), preferred_element_type=F32)
        acc = acc + jnp.dot(b_ref[...], wb_ref[...].astype(BF16), preferred_element_type=F32)
        y_ref[...] = x_ref[...] + acc

    return pl.pallas_call(
        body,
        out_shape=jax.ShapeDtypeStruct((t, d), F32),
        grid=(t // tm, d // tn),
        in_specs=[
            pl.BlockSpec((tm, WIDTH), lambda i, j: (i, 0)),
            pl.BlockSpec((tm, WIDTH), lambda i, j: (i, 0)),
            pl.BlockSpec((None, WIDTH, tn), lambda i, j: (layer, 0, j)),
            pl.BlockSpec((None, WIDTH, tn), lambda i, j: (layer, 1, j)),
            pl.BlockSpec((tm, tn), lambda i, j: (i, j)),
        ],
        out_specs=pl.BlockSpec((tm, tn), lambda i, j: (i, j)),
        compiler_params=_cparams("parallel", "parallel"),
        name=name,
    )(o_a, o_b, w_out, w_out, x)


def _rope_tables(pos):
    half = ROPE_DIM // 2
    inv_freq = 1.0 / (ROPE_THETA ** (jnp.arange(half, dtype=F32) / half))
    ang = pos.astype(F32)[:, None] * inv_freq[None, :]
    cos, sin = jnp.cos(ang), jnp.sin(ang)
    cos64 = jnp.concatenate([cos, cos], axis=-1)
    sin64 = jnp.concatenate([-sin, sin], axis=-1)
    return jnp.concatenate([cos64, cos64], axis=-1), jnp.concatenate([sin64, sin64], axis=-1)


def _prep_w_in_call(w_in_t, layer, name):
    _, n_in, d = w_in_t.shape
    tc = _row_tile(d, 256)
    src = {}
    o = 0
    for seg, size in (("q", WIDTH), ("k", WIDTH), ("v", WIDTH), ("f", HEADS), ("za", WIDTH),
                      ("cq", Q_LORA), ("ckv", KV_LORA), ("kpe", ROPE_DIM), ("zb", WIDTH)):
        src[seg] = (o, size)
        o += size
    assert o == n_in
    dst = (("q", COL_Q), ("k", COL_K), ("v", COL_V), ("za", COL_GATE), ("zb", COL_GATE + WIDTH), ("cq", COL_CQ),
           ("ckv", COL_CKV), ("kpe", COL_SMALL), ("f", COL_SMALL + LANES))

    def body(x_ref, o_ref):
        for seg, d0 in dst:
            s0, size = src[seg]
            o_ref[d0:d0 + size, :] = x_ref[s0:s0 + size, :].astype(BF16)
        pad0 = COL_SMALL + ROPE_DIM
        o_ref[pad0:COL_SMALL + LANES, :] = jnp.zeros((LANES - ROPE_DIM, tc), BF16)
        pad1 = COL_SMALL + LANES + HEADS
        o_ref[pad1:COL_SMALL + N_SMALL, :] = jnp.zeros((LANES - HEADS, tc), BF16)

    return pl.pallas_call(
        body,
        out_shape=jax.ShapeDtypeStruct((COL_SMALL + N_SMALL, d), BF16),
        grid=(d // tc,),
        in_specs=[pl.BlockSpec((None, n_in, tc), lambda i: (layer, 0, i))],
        out_specs=pl.BlockSpec((COL_SMALL + N_SMALL, tc), lambda i: (0, i)),
        compiler_params=_cparams("parallel"),
        name=name,
    )(w_in_t)


def _prep_w_qb_call(w_qb, layer, name):
    _, k, n = w_qb.shape
    per_head = NOPE_DIM + ROPE_DIM
    tr = _row_tile(k, 256)

    def body(x_ref, o_ref):
        for hh in range(HEADS):
            o_ref[:, hh * NOPE_DIM:(hh + 1) * NOPE_DIM] = x_ref[:, hh * per_head:hh * per_head + NOPE_DIM].astype(BF16)
        for p in range(HEADS // 2):
            pe = [x_ref[:, hh * per_head + NOPE_DIM:(hh + 1) * per_head] for hh in (2 * p, 2 * p + 1)]
            o_ref[:, WIDTH + p * LANES:WIDTH + (p + 1) * LANES] = jnp.concatenate(pe, axis=1).astype(BF16)

    return pl.pallas_call(
        body,
        out_shape=jax.ShapeDtypeStruct((k, n), BF16),
        grid=(k // tr,),
        in_specs=[pl.BlockSpec((None, tr, n), lambda i: (layer, i, 0))],
        out_specs=pl.BlockSpec((tr, n), lambda i: (i, 0)),
        compiler_params=_cparams("parallel"),
        name=name,
    )(w_qb)


def _layer_params(l, g_norm, b_f, g_q_fox, g_k_fox, g_cq, g_qn, g_qp, g_ckv, g_kp, g_kn):
    row = lambda a: a.reshape(1, -1).astype(F32)
    pad = lambda a: jnp.concatenate([row(a), jnp.zeros((1, LANES - a.shape[-1]), F32)], axis=1)
    return dict(
        g_norm=row(g_norm[l]), b_f_row=pad(b_f[l]), g_q=row(g_q_fox[l]), g_k=row(g_k_fox[l]), g_cq=row(g_cq[l]),
        g_qn=row(g_qn[l]), g_qp2=jnp.concatenate([row(g_qp[l]), row(g_qp[l])], axis=1),
        g_ckv=row(g_ckv[l]), g_kp=pad(g_kp[l]), g_kn=row(g_kn[l]),
    )


def _input_stage(x, w, w_qb, p, cos, sin, layer, prev, depth, nb, sl, prompt, tag):
    h = _rmsnorm_call(x, p["g_norm"], f"norm_{tag}")
    prev = prev or {}
    r = {}
    r["q"] = _proj_heads(h, w, COL_Q, p["g_q"], FOX_SCALE, nb, sl, prompt, f"proj_q_{tag}",
                         w_rows_are_outputs=True)
    k_out = _proj_kv_fox(h, w, COL_K, p["g_k"], layer, prev.get("k"), depth, nb, sl, prompt, f"proj_k_{tag}")
    v_out = _proj_kv_fox(h, w, COL_V, None, layer, prev.get("v"), depth, nb, sl, prompt, f"proj_v_{tag}")
    r["k"], r["v"] = k_out[0], v_out[0]
    if prompt:
        r["k_bf"], r["v_bf"] = k_out[1], v_out[1]
    r["gate"] = _proj_gate(h, w, f"proj_gate_{tag}")
    cq = _proj_cq(h, w, p["g_cq"], f"proj_cq_{tag}")
    r["ckv"], r["ckv_bf"] = _proj_ckv(h, w, p["g_ckv"], layer, prev.get("ckv"), depth, f"proj_ckv_{tag}")
    r["kpe"], r["kpe2"], r["logf"], r["cum"] = _proj_small(
        h, w, p["g_kp"], p["b_f_row"], cos, sin, layer, prev.get("kpe"), prev.get("logf"), depth, nb, sl, prompt,
        f"proj_small_{tag}")
    r["q_nope"] = _proj_heads(cq, w_qb, 0, p["g_qn"], MLA_SCALE, nb, sl, prompt, f"proj_qn_{tag}")
    r["q_pe"] = _proj_q_pe(cq, w_qb, p["g_qp2"], cos, sin, nb, sl, prompt, f"proj_qp_{tag}")
    return r


def kernel(x_prompt, x_sample, cache_fox_k, cache_fox_v, cache_fox_logf, cache_mla_ckv, cache_mla_kpe, g_norm,
           w_in, b_f, g_q_fox, g_k_fox, g_cq, w_qb, g_qn, g_qp, g_ckv, g_kp, w_kvb, g_kn, w_out):
    nb, sl, d = x_prompt.shape
    nbd, sd, _ = x_sample.shape
    depth = w_in.shape[0]
    past = cache_fox_k.shape[2]
    assert sl % CHUNK == 0 and past % CHUNK == 0 and sd <= CHUNK, "chunk-aligned streaming shapes only"

    cos_p, sin_p = _rope_tables(jnp.arange(sl, dtype=jnp.int32))
    cos_p, sin_p = jnp.tile(cos_p, (nb, 1)), jnp.tile(sin_p, (nb, 1))
    cos_s, sin_s = _rope_tables(past + jnp.arange(sd, dtype=jnp.int32))
    cos_s, sin_s = jnp.tile(cos_s, (nbd, 1)), jnp.tile(sin_s, (nbd, 1))

    ck2 = cache_fox_k.reshape(-1, HEAD_DIM)
    cv2 = cache_fox_v.reshape(-1, HEAD_DIM)
    logf_t = jnp.swapaxes(cache_fox_logf, 2, 3)
    w_in_t = jnp.swapaxes(w_in, 1, 2)
    cckv = cache_mla_ckv.reshape(depth * nbd, past, KV_LORA)
    ckpe = jnp.swapaxes(cache_mla_kpe, 2, 3).reshape(depth * nbd, ROPE_DIM, past)

    y_p = x_prompt.reshape(nb * sl, d)
    y_s = x_sample.reshape(nbd * sd, d)
    prev_p, prev_s = None, None
    for l in range(depth):
        p = _layer_params(l, g_norm, b_f, g_q_fox, g_k_fox, g_cq, g_qn, g_qp, g_ckv, g_kp, g_kn)
        w = _prep_w_in_call(w_in_t, l, f"prep_w_in_{l}")
        wq = _prep_w_qb_call(w_qb, l, f"prep_w_qb_{l}")
        rp = _input_stage(y_p, w, wq, p, cos_p, sin_p, l, prev_p, depth, nb, sl, True, f"p{l}")
        kn_p, v_p = _proj_kv_up(rp["ckv_bf"], w_kvb, l, p["g_kn"], nb, sl, f"kv_up_p{l}")
        o_a = _flash_call([rp["q"]], rp["k_bf"], rp["cum"].reshape(nb, sl, LANES), rp["v_bf"], rp["gate"], 0,
                          nb, sl, True, f"fox_attn_p{l}")
        o_b = _flash_call([rp["q_nope"], rp["q_pe"]], kn_p, rp["kpe2"].reshape(nb, sl, LANES), v_p, rp["gate"],
                          HEADS, nb, sl, False, f"mla_attn_p{l}")
        y_p = _out_proj_call(o_a, o_b, w_out, l, y_p, f"out_proj_p{l}")
        prev_p = rp
        rs = _input_stage(y_s, w, wq, p, cos_s, sin_s, l, prev_s, depth, nbd, sd, False, f"s{l}")
        o_a = _fox_decode_call(rs["q"], ck2, cv2, rs["k"], rs["v"], l, logf_t, rs["cum"], rs["gate"], nbd, sd,
                               past, f"fox_attn_s{l}")
        o_b = _mla_decode_call(rs["q_nope"], rs["q_pe"], cckv, ckpe, rs["ckv_bf"], rs["kpe2"], l, w_kvb,
                               p["g_kn"].reshape(HEAD_DIM, 1), rs["gate"], nbd, sd, past, f"mla_attn_s{l}")
        y_s = _out_proj_call(o_a, o_b, w_out, l, y_s, f"out_proj_s{l}")
        prev_s = rs

    hk = (HEADS, HEAD_DIM)
    return (y_p.reshape(nb, sl, d), y_s.reshape(nbd, sd, d),
            prev_p["k"].reshape((depth, nb, sl) + hk), prev_p["v"].reshape((depth, nb, sl) + hk),
            prev_p["logf"].reshape(depth, nb, sl, HEADS), prev_p["ckv"].reshape(depth, nb, sl, KV_LORA),
            prev_p["kpe"].reshape(depth, nb, sl, ROPE_DIM),
            prev_s["k"].reshape((depth, nbd, sd) + hk), prev_s["v"].reshape((depth, nbd, sd) + hk),
            prev_s["logf"].reshape(depth, nbd, sd, HEADS), prev_s["ckv"].reshape(depth, nbd, sd, KV_LORA),
            prev_s["kpe"].reshape(depth, nbd, sd, ROPE_DIM))
```

```python
import math

import jax
import jax.numpy as jnp
from jax import lax
from jax.experimental import pallas as pl
from jax.experimental.pallas import tpu as pltpu

F32 = jnp.float32
BF16 = jnp.bfloat16

CHUNK = 64
HEADS = 16
HEAD_DIM = 128
WIDTH = HEADS * HEAD_DIM
NOPE_DIM = 128
ROPE_DIM = 64
Q_LORA = 1024
KV_LORA = 512
ROPE_THETA = 10000.0
EPS = 1e-6
LOG2E = math.log2(math.e)
FOX_SCALE = LOG2E / math.sqrt(HEAD_DIM)
MLA_SCALE = LOG2E / math.sqrt(NOPE_DIM + ROPE_DIM)
LANES = 128
SUBLANES = 8
NEW_PAD = 128
VMEM_LIMIT_BYTES = 52 * 1024 * 1024
NEG_INF = float("-inf")

COL_Q, COL_K, COL_V, COL_GATE = 0, WIDTH, 2 * WIDTH, 3 * WIDTH
COL_CQ = 5 * WIDTH
COL_CKV = COL_CQ + Q_LORA
COL_SMALL = COL_CKV + KV_LORA
N_SMALL = 2 * LANES


def _cparams(*sem):
    return pltpu.CompilerParams(dimension_semantics=sem, vmem_limit_bytes=VMEM_LIMIT_BYTES)


def _row_tile(t, cap):
    if t <= cap:
        return t
    for c in range(cap, 15, -16):
        if t % c == 0:
            return c
    return t


def _rms(x, n=None, axis=-1):
    n = x.shape[axis] if n is None else n
    ms = jnp.sum(x * x, axis=axis, keepdims=True) * (1.0 / n)
    return x * lax.rsqrt(ms + EPS)


def _silu(x):
    return x * (1.0 / (1.0 + jnp.exp(-x)))


def _log_sigmoid(x):
    return jnp.minimum(x, 0.0) - jnp.log1p(jnp.exp(-jnp.abs(x)))


def _scan(x, axis, length, pos):
    sh = 1
    while sh < length:
        x = x + jnp.where(pos >= sh, pltpu.roll(x, sh, axis), 0.0)
        sh *= 2
    return x


def _rmsnorm_call(x, g, name):
    t, d = x.shape
    tm = _row_tile(t, 256)

    def body(x_ref, g_ref, o_ref):
        o_ref[...] = (_rms(x_ref[...]) * g_ref[...]).astype(BF16)

    return pl.pallas_call(
        body,
        out_shape=jax.ShapeDtypeStruct((t, d), BF16),
        grid=(t // tm,),
        in_specs=[pl.BlockSpec((tm, d), lambda i: (i, 0)), pl.BlockSpec((1, d), lambda i: (0, 0))],
        out_specs=pl.BlockSpec((tm, d), lambda i: (i, 0)),
        compiler_params=_cparams("parallel"),
        name=name,
    )(x, g)


def _proj_call(x, w, col0, n, tm, tn, extras, extra_specs, out_shapes, out_specs, epilogue, name,
               alias_inputs=(), scratch_shapes=(), layer=None, w_rows_are_outputs=False, row_chunks=1,
               whole_tile_epilogue=False):
    t, k = x.shape
    assert col0 % tn == 0 and n % tn == 0 and t % tm == 0
    cb0 = col0 // tn
    n_ex, n_al, n_out = len(extras), len(alias_inputs), len(out_shapes)
    contract = (((1,), (1,)), ((), ())) if w_rows_are_outputs else (((1,), (0,)), ((), ()))

    assert tm % (16 * row_chunks) == 0
    nr = tm // row_chunks

    def body(x_ref, w_ref, *refs):
        ex = refs[:n_ex]
        outs = refs[n_ex + n_al:n_ex + n_al + n_out]
        scr = refs[n_ex + n_al + n_out:]
        wv = w_ref[...].astype(BF16)
        accs = []
        for c in range(row_chunks):
            rows = slice(c * nr, (c + 1) * nr)
            acc = lax.dot_general(x_ref[rows, :], wv, contract, preferred_element_type=F32)
            if whole_tile_epilogue:
                accs.append(acc)
            else:
                epilogue(acc, rows, x_ref, ex, outs, scr)
        if whole_tile_epilogue:
            epilogue(jnp.concatenate(accs, axis=0), slice(0, tm), x_ref, ex, outs, scr)

    if w_rows_are_outputs:
        assert layer is None
        w_spec = pl.BlockSpec((tn, k), lambda i, j: (cb0 + j, 0))
    elif layer is None:
        w_spec = pl.BlockSpec((k, tn), lambda i, j: (0, cb0 + j))
    else:
        w_spec = pl.BlockSpec((None, k, tn), lambda i, j: (layer, 0, cb0 + j))
    in_specs = [pl.BlockSpec((tm, k), lambda i, j: (i, 0)), w_spec]
    in_specs += list(extra_specs)
    in_specs += [pl.BlockSpec(memory_space=pl.ANY) for _ in alias_inputs]
    aliases = {2 + n_ex + a: out_idx for a, (_, out_idx) in enumerate(alias_inputs)}
    return pl.pallas_call(
        body,
        out_shape=out_shapes,
        grid=(t // tm, n // tn),
        in_specs=in_specs,
        out_specs=out_specs,
        scratch_shapes=list(scratch_shapes),
        input_output_aliases=aliases,
        compiler_params=_cparams("arbitrary", "arbitrary"),
        name=name,
    )(x, w, *extras, *[a for a, _ in alias_inputs])


def _chunks(tm):
    return 2 if tm % 512 == 0 else 1


def _const_spec(shape):
    nd = len(shape)
    return pl.BlockSpec(shape, lambda i, j: (0,) * nd)


def _head_out(nb, sl, tm, nh_t, head_major):
    if head_major:
        nsb = sl // tm
        shape = jax.ShapeDtypeStruct((nb, HEADS, sl, HEAD_DIM), BF16)
        spec = pl.BlockSpec((None, nh_t, tm, HEAD_DIM), lambda i, j: (i // nsb, j, i % nsb, 0))
    else:
        shape = jax.ShapeDtypeStruct((nb * sl, WIDTH), BF16)
        spec = pl.BlockSpec((tm, nh_t * HEAD_DIM), lambda i, j: (i, j))
    return shape, spec


def _store_head(o_ref, hh, rows, val, head_major):
    if head_major:
        o_ref[hh, rows, :] = val
    else:
        o_ref[rows, hh * HEAD_DIM:(hh + 1) * HEAD_DIM] = val


def _proj_heads(h, w, col0, g, scale, nb, sl, head_major, name, w_rows_are_outputs=False):
    t = h.shape[0]
    tm, tn = _row_tile(min(t, sl) if head_major else t, 1024), 1024
    nh_t = tn // HEAD_DIM
    shape, spec = _head_out(nb, sl, tm, nh_t, head_major)

    def epi(acc, rows, x_ref, ex, outs, scr):
        g_ref, = ex
        for hh in range(nh_t):
            v = _rms(acc[:, hh * HEAD_DIM:(hh + 1) * HEAD_DIM]) * g_ref[...] * scale
            _store_head(outs[0], hh, rows, v.astype(BF16), head_major)

    return _proj_call(h, w, col0, WIDTH, tm, tn, [g], [_const_spec((1, HEAD_DIM))], [shape], [spec], epi, name,
                      w_rows_are_outputs=w_rows_are_outputs, row_chunks=_chunks(tm))[0]


def _proj_kv_fox(h, w, col0, g, layer, prev, depth, nb, sl, head_major, name):
    t = h.shape[0]
    tm = _row_tile(min(t, sl) if head_major else t, 1024)
    tn = 512 if tm > 256 else 1024
    nh_t = tn // HEAD_DIM
    f_shape = jax.ShapeDtypeStruct((depth, t * HEADS, HEAD_DIM), F32)
    f_spec = pl.BlockSpec((None, tm * HEADS, HEAD_DIM), lambda i, j: (layer, i, 0))
    shapes, specs = [f_shape], [f_spec]
    if head_major:
        b_shape, b_spec = _head_out(nb, sl, tm, nh_t, True)
        shapes.append(b_shape)
        specs.append(b_spec)
    extras = [] if g is None else [g]
    especs = [] if g is None else [_const_spec((1, HEAD_DIM))]

    def epi(acc, rows, x_ref, ex, outs, scr):
        j = pl.program_id(1)
        for hh in range(nh_t):
            v = acc[:, hh * HEAD_DIM:(hh + 1) * HEAD_DIM]
            if g is not None:
                v = _rms(v) * ex[0][...]
            nr = rows.stop - rows.start
            outs[0][pl.ds(rows.start * HEADS + j * nh_t + hh, nr, stride=HEADS), :] = v
            if head_major:
                outs[1][hh, rows, :] = v.astype(BF16)

    alias = [] if prev is None else [(prev, 0)]
    return _proj_call(h, w, col0, WIDTH, tm, tn, extras, especs, shapes, specs, epi, name, alias_inputs=alias,
                      w_rows_are_outputs=True, row_chunks=_chunks(tm))


def _proj_gate(h, w, name):
    t = h.shape[0]
    tm, tn = _row_tile(t, 1024), 1024

    def epi(acc, rows, x_ref, ex, outs, scr):
        outs[0][rows, :] = _silu(acc).astype(BF16)

    return _proj_call(h, w, COL_GATE, 2 * WIDTH, tm, tn, [], [], [jax.ShapeDtypeStruct((t, 2 * WIDTH), BF16)],
                      [pl.BlockSpec((tm, tn), lambda i, j: (i, j))], epi, name, w_rows_are_outputs=True,
                      row_chunks=_chunks(tm))[0]


def _proj_cq(h, w, g, name):
    t = h.shape[0]
    tm, tn = _row_tile(t, 1024), Q_LORA

    def epi(acc, rows, x_ref, ex, outs, scr):
        outs[0][rows, :] = (_rms(acc) * ex[0][...]).astype(BF16)

    return _proj_call(h, w, COL_CQ, Q_LORA, tm, tn, [g], [_const_spec((1, Q_LORA))],
                      [jax.ShapeDtypeStruct((t, Q_LORA), BF16)], [pl.BlockSpec((tm, tn), lambda i, j: (i, 0))],
                      epi, name, w_rows_are_outputs=True, row_chunks=_chunks(tm))[0]


def _proj_ckv(h, w, g, layer, prev, depth, name):
    t = h.shape[0]
    tm, tn = _row_tile(t, 1024), KV_LORA

    def epi(acc, rows, x_ref, ex, outs, scr):
        v = _rms(acc) * ex[0][...]
        outs[0][rows, :] = v
        outs[1][rows, :] = v.astype(BF16)

    shapes = [jax.ShapeDtypeStruct((depth, t, KV_LORA), F32), jax.ShapeDtypeStruct((t, KV_LORA), BF16)]
    specs = [pl.BlockSpec((None, tm, tn), lambda i, j: (layer, i, 0)), pl.BlockSpec((tm, tn), lambda i, j: (i, 0))]
    alias = [] if prev is None else [(prev, 0)]
    return _proj_call(h, w, COL_CKV, KV_LORA, tm, tn, [g], [_const_spec((1, KV_LORA))], shapes, specs, epi, name,
                      alias_inputs=alias, w_rows_are_outputs=True, row_chunks=_chunks(tm))


def _swap_halves(x):
    lane = lax.broadcasted_iota(jnp.int32, x.shape, 1)
    return jnp.where((lane % ROPE_DIM) < ROPE_DIM // 2, pltpu.roll(x, LANES - ROPE_DIM // 2, 1),
                     pltpu.roll(x, ROPE_DIM // 2, 1))


def _proj_small(h, w, g_kp, b_f_row, cos, sin, layer, prev_kpe, prev_logf, depth, nb, sl, prompt, name):
    t = h.shape[0]
    tm = _row_tile(min(t, sl), 1024) if prompt else t
    nsb = sl // tm if prompt else 1
    seqs_per_blk = 1 if prompt else tm // sl

    def epi(acc, rows, x_ref, ex, outs, scr):
        g_ref, bfr_ref, cos_ref, sin_ref = ex
        kpe_ref, kpe2_ref, logf_ref, cum_ref = outs
        i = pl.program_id(0)
        kp = _rms(acc[:, :LANES], ROPE_DIM) * g_ref[...]
        kp = kp * cos_ref[...] + _swap_halves(kp) * sin_ref[...]
        kpe_ref[...] = kp[:, :ROPE_DIM]
        kpe2_ref[...] = (kp + pltpu.roll(kp, ROPE_DIM, 1)).astype(BF16)
        lf = _log_sigmoid(acc[:, LANES:2 * LANES] + bfr_ref[...])
        logf_ref[...] = lf[:, :HEADS]
        if prompt:
            carry_ref, = scr
            lane = lax.broadcasted_iota(jnp.int32, lf.shape, 1)
            row = lax.broadcasted_iota(jnp.int32, lf.shape, 0)
            c = _scan(jnp.where(lane < HEADS, lf, 0.0), 0, tm, row)
            if nsb > 1:
                @pl.when(i % nsb == 0)
                def _():
                    carry_ref[...] = jnp.zeros_like(carry_ref)
                c = c + carry_ref[0:1, :]
                carry_ref[...] = jnp.broadcast_to(c[tm - 1:tm, :], carry_ref.shape)
            neg = c * (-LOG2E)
            hi = neg.astype(BF16).astype(F32)
            mid = (neg - hi).astype(BF16).astype(F32)
            lo = neg - hi - mid
            cum_ref[...] = (hi + pltpu.roll(mid, HEADS, 1) + pltpu.roll(lo, 2 * HEADS, 1)).astype(BF16)
        else:
            c = lf.T[:HEADS, :]
            pos = lax.broadcasted_iota(jnp.int32, c.shape, 1) % sl
            c = _scan(c, 1, sl, pos) * (-LOG2E)
            cum_ref[...] = jnp.zeros(cum_ref.shape, F32)
            for s in range(seqs_per_blk):
                cum_ref[s, :, 0:sl] = c[:, s * sl:(s + 1) * sl]

    if prompt:
        cum_shape = jax.ShapeDtypeStruct((t, LANES), BF16)
        cum_spec = pl.BlockSpec((tm, LANES), lambda i, j: (i, 0))
        scratch = [pltpu.VMEM((SUBLANES, LANES), F32)]
    else:
        assert sl <= LANES
        cum_shape = jax.ShapeDtypeStruct((nb, HEADS, LANES), F32)
        cum_spec = pl.BlockSpec((seqs_per_blk, HEADS, LANES), lambda i, j: (i, 0, 0))
        scratch = []
    shapes = [jax.ShapeDtypeStruct((depth, t, ROPE_DIM), F32), jax.ShapeDtypeStruct((t, LANES), BF16),
              jax.ShapeDtypeStruct((depth, t, HEADS), F32), cum_shape]
    specs = [pl.BlockSpec((None, tm, ROPE_DIM), lambda i, j: (layer, i, 0)),
             pl.BlockSpec((tm, LANES), lambda i, j: (i, 0)),
             pl.BlockSpec((None, tm, HEADS), lambda i, j: (layer, i, 0)), cum_spec]
    extras = [g_kp, b_f_row, cos, sin]
    especs = [_const_spec((1, LANES)), _const_spec((1, LANES)),
              pl.BlockSpec((tm, LANES), lambda i, j: (i, 0)), pl.BlockSpec((tm, LANES), lambda i, j: (i, 0))]
    alias = []
    if prev_kpe is not None:
        alias = [(prev_kpe, 0), (prev_logf, 2)]
    return _proj_call(h, w, COL_SMALL, N_SMALL, tm, N_SMALL, extras, especs, shapes, specs, epi, name,
                      alias_inputs=alias, scratch_shapes=scratch, w_rows_are_outputs=True,
                      row_chunks=_chunks(tm), whole_tile_epilogue=True)


def _proj_q_pe(cq, w, g2, cos, sin, nb, sl, head_major, name):
    t = cq.shape[0]
    tm, tn = _row_tile(min(t, sl) if head_major else t, 1024), HEADS * ROPE_DIM
    shape, spec = _head_out(nb, sl, tm, HEADS, head_major)

    def epi(acc, rows, x_ref, ex, outs, scr):
        g_ref, cos_ref, sin_ref = ex
        lane = lax.broadcasted_iota(jnp.int32, (acc.shape[0], LANES), 1)
        lo = lane < ROPE_DIM
        cos_t, sin_t = cos_ref[rows, :], sin_ref[rows, :]
        for p in range(HEADS // 2):
            x = acc[:, p * LANES:(p + 1) * LANES]
            x2 = x * x
            ms_lo = jnp.sum(jnp.where(lo, x2, 0.0), axis=-1, keepdims=True)
            ms_hi = jnp.sum(jnp.where(lo, 0.0, x2), axis=-1, keepdims=True)
            ms = jnp.where(lo, ms_lo, ms_hi) * (1.0 / ROPE_DIM)
            y = x * lax.rsqrt(ms + EPS) * g_ref[...]
            y = (y * cos_t + _swap_halves(y) * sin_t) * MLA_SCALE
            _store_head(outs[0], 2 * p, rows, jnp.where(lo, y, 0.0).astype(BF16), head_major)
            _store_head(outs[0], 2 * p + 1, rows, jnp.where(lo, 0.0, y).astype(BF16), head_major)

    especs = [_const_spec((1, LANES)), pl.BlockSpec((tm, LANES), lambda i, j: (i, 0)),
              pl.BlockSpec((tm, LANES), lambda i, j: (i, 0))]
    return _proj_call(cq, w, WIDTH, tn, tm, tn, [g2, cos, sin], especs, [shape], [spec], epi, name,
                      row_chunks=_chunks(tm))[0]


def _proj_kv_up(ckv, w, layer, g, nb, sl, name):
    t = ckv.shape[0]
    tm, tn = _row_tile(min(t, sl), 1024), 1024
    nh_t = tn // (2 * HEAD_DIM)
    nsb = sl // tm
    shape = jax.ShapeDtypeStruct((nb, HEADS, sl, HEAD_DIM), BF16)
    spec = pl.BlockSpec((None, nh_t, tm, HEAD_DIM), lambda i, j: (i // nsb, j, i % nsb, 0))

    def epi(acc, rows, x_ref, ex, outs, scr):
        for hh in range(nh_t):
            base = hh * 2 * HEAD_DIM
            outs[0][hh, rows, :] = (_rms(acc[:, base:base + HEAD_DIM]) * ex[0][...]).astype(BF16)
            outs[1][hh, rows, :] = acc[:, base + HEAD_DIM:base + 2 * HEAD_DIM].astype(BF16)

    return _proj_call(ckv, w, 0, 2 * WIDTH, tm, tn, [g], [_const_spec((1, HEAD_DIM))], [shape, shape],
                      [spec, spec], epi, name, layer=layer, row_chunks=_chunks(tm))


FLASH_HEADS_PER_STEP = 2


def _flash_call(q_parts, k_head, k_shared, v, gate, gate_col0, nb, sl, fox, name):
    tq = _row_tile(sl, 256)
    nqb = sl // tq
    t = nb * sl
    n_q = len(q_parts)

    hps = FLASH_HEADS_PER_STEP
    chains = [(hh, qi) for hh in range(hps) for qi in range(nqb)]

    def body(*refs):
        q_refs = refs[:n_q]
        kh_ref, ks_ref, v_ref, g_ref, o_ref = refs[n_q:]
        hp = pl.program_id(1)
        key = lax.broadcasted_iota(jnp.int32, (tq, tq), 0)
        qry = lax.broadcasted_iota(jnp.int32, (tq, tq), 1)
        allowed = (key <= qry) if fox else (key // CHUNK <= qry // CHUNK)
        if fox:
            lane = lax.broadcasted_iota(jnp.int32, (tq, LANES), 1)
            selectors = [jnp.where((lane % HEADS == hp * hps + hh) & (lane < 3 * HEADS), 1.0, 0.0).astype(BF16)
                         for hh in range(hps)]

        def q_block(hh, qi):
            rows = slice(qi * tq, (qi + 1) * tq)
            parts = [r[hh, rows, :] for r in q_refs] + ([selectors[hh]] if fox else [])
            return jnp.concatenate(parts, axis=-1)

        def k_block(hh, kb):
            rows = slice(kb * tq, (kb + 1) * tq)
            return jnp.concatenate([kh_ref[hh, rows, :], ks_ref[rows, :]], axis=-1)

        def score(rnd, hh, qi):
            s = lax.dot_general(k_block(hh, qi - rnd), q_block(hh, qi), (((1,), (1,)), ((), ())),
                                preferred_element_type=F32)
            return jnp.where(allowed, s, NEG_INF) if rnd == 0 else s

        state = {c: (jnp.full((1, tq), NEG_INF, F32), jnp.zeros((1, tq), F32), jnp.zeros((HEAD_DIM, tq), F32))
                 for c in chains}
        s_cur = {c: score(0, *c) for c in chains}
        for rnd in range(nqb):
            s_next = {}
            for hh, qi in chains:
                if qi < rnd:
                    continue
                if qi >= rnd + 1 and rnd + 1 < nqb:
                    s_next[hh, qi] = score(rnd + 1, hh, qi)
                m, l, acc = state[hh, qi]
                s = s_cur[hh, qi]
                m_new = jnp.maximum(m, jnp.max(s, axis=0, keepdims=True))
                alpha = jnp.exp2(m - m_new)
                p = jnp.exp2(s - m_new)
                l = alpha * l + jnp.sum(p, axis=0, keepdims=True)
                kb = qi - rnd
                pv = lax.dot_general(v_ref[hh, kb * tq:(kb + 1) * tq, :], p.astype(BF16),
                                     (((0,), (0,)), ((), ())), preferred_element_type=F32)
                state[hh, qi] = (m_new, l, alpha * acc + pv)
            s_cur = s_next
        for hh, qi in chains:
            m, l, acc = state[hh, qi]
            rows = slice(qi * tq, (qi + 1) * tq)
            cols = slice(hh * HEAD_DIM, (hh + 1) * HEAD_DIM)
            o_ref[rows, cols] = ((acc / l).T * g_ref[rows, cols].astype(F32)).astype(BF16)

    head_spec = pl.BlockSpec((None, hps, sl, HEAD_DIM), lambda b, h: (b, h, 0, 0))
    gate_cb0 = gate_col0 // hps
    in_specs = [head_spec for _ in q_parts] + [
        head_spec, pl.BlockSpec((None, sl, LANES), lambda b, h: (b, 0, 0)), head_spec,
        pl.BlockSpec((None, sl, hps * HEAD_DIM), lambda b, h: (b, 0, gate_cb0 + h))]
    return pl.pallas_call(
        body,
        out_shape=jax.ShapeDtypeStruct((nb, sl, WIDTH), BF16),
        grid=(nb, HEADS // hps),
        in_specs=in_specs,
        out_specs=pl.BlockSpec((None, sl, hps * HEAD_DIM), lambda b, h: (b, 0, h)),
        compiler_params=_cparams("parallel", "parallel"),
        name=name,
    )(*q_parts, k_head, k_shared, v, gate.reshape(nb, sl, 2 * WIDTH)).reshape(t, WIDTH)


def _pad_rows(x, rows):
    return jnp.concatenate([x, jnp.zeros((rows - x.shape[0], x.shape[1]), x.dtype)], axis=0)


def _fox_decode_call(q, cache_k, cache_v, new_k, new_v, layer, logf_t, cum_new, gate, nb, sd, past, name):
    tk = _row_tile(past, 512)
    nkc = past // tk
    ts = nb * sd
    rows = HEADS * sd

    def body(q_ref, ck_ref, cv_ref, nk_ref, nv_ref, lf_ref, cn_ref, g_ref, o_ref, cum_ref, m_ref, l_ref, acc_ref):
        kc = pl.program_id(1)

        @pl.when(kc == 0)
        def _():
            c = lf_ref[...]
            cum_ref[...] = _scan(c, 1, past, lax.broadcasted_iota(jnp.int32, c.shape, 1)) * (-LOG2E)
            m_ref[...] = jnp.full(m_ref.shape, NEG_INF, F32)
            l_ref[...] = jnp.zeros(l_ref.shape, F32)
            acc_ref[...] = jnp.zeros(acc_ref.shape, F32)

        def attend(keys, values, bias, mask):
            s = []
            for hh in range(HEADS):
                qh = q_ref[:, hh * HEAD_DIM:(hh + 1) * HEAD_DIM]
                sh = lax.dot_general(qh, keys(hh), (((1,), (1,)), ((), ())), preferred_element_type=F32)
                sh = sh + bias(hh)
                s.append(sh if mask is None else jnp.where(mask, sh, NEG_INF))
            s = jnp.concatenate(s, axis=0)
            m_old = m_ref[...]
            m_new = jnp.maximum(m_old, jnp.max(s, axis=-1, keepdims=True))
            alpha = jnp.exp2(m_old - m_new)
            p = jnp.exp2(s - m_new)
            l_ref[...] = alpha * l_ref[...] + jnp.sum(p, axis=-1, keepdims=True)
            m_ref[...] = m_new
            pb = p.astype(BF16)
            pv = [jnp.dot(pb[hh * sd:(hh + 1) * sd, :], values(hh), preferred_element_type=F32)
                  for hh in range(HEADS)]
            acc_ref[...] = alpha * acc_ref[...] + jnp.concatenate(pv, axis=0)

        def by_head(ref, n):
            return jnp.swapaxes(ref[...].astype(BF16).reshape(n, HEADS, HEAD_DIM), 0, 1)

        start = pl.multiple_of(kc * tk, tk)
        kk, vv = by_head(ck_ref, tk), by_head(cv_ref, tk)
        attend(lambda hh: kk[hh], lambda hh: vv[hh], lambda hh: cum_ref[hh:hh + 1, pl.ds(start, tk)], None)

        @pl.when(kc == nkc - 1)
        def _():
            row = lax.broadcasted_iota(jnp.int32, (sd, NEW_PAD), 0)
            col = lax.broadcasted_iota(jnp.int32, (sd, NEW_PAD), 1)
            nk, nv = by_head(nk_ref, sd), by_head(nv_ref, sd)
            attend(lambda hh: _pad_rows(nk[hh], NEW_PAD), lambda hh: _pad_rows(nv[hh], NEW_PAD),
                   lambda hh: cum_ref[hh:hh + 1, past - 1:past] + cn_ref[hh:hh + 1, :], col <= row)
            out = acc_ref[...] / l_ref[...]
            for hh in range(HEADS):
                gh = g_ref[:, hh * HEAD_DIM:(hh + 1) * HEAD_DIM].astype(F32)
                o_ref[:, hh * HEAD_DIM:(hh + 1) * HEAD_DIM] = (out[hh * sd:(hh + 1) * sd, :] * gh).astype(BF16)

    cache_spec = pl.BlockSpec((tk * HEADS, HEAD_DIM), lambda b, kc: ((layer * nb + b) * nkc + kc, 0))
    new_spec = pl.BlockSpec((None, sd * HEADS, HEAD_DIM), lambda b, kc: (layer, b, 0))
    return pl.pallas_call(
        body,
        out_shape=jax.ShapeDtypeStruct((ts, WIDTH), BF16),
        grid=(nb, nkc),
        in_specs=[
            pl.BlockSpec((sd, WIDTH), lambda b, kc: (b, 0)),
            cache_spec, cache_spec, new_spec, new_spec,
            pl.BlockSpec((None, None, HEADS, past), lambda b, kc: (layer, b, 0, 0)),
            pl.BlockSpec((None, HEADS, NEW_PAD), lambda b, kc: (b, 0, 0)),
            pl.BlockSpec((sd, WIDTH), lambda b, kc: (b, 0)),
        ],
        out_specs=pl.BlockSpec((sd, WIDTH), lambda b, kc: (b, 0)),
        scratch_shapes=[pltpu.VMEM((HEADS, past), F32), pltpu.VMEM((rows, 1), F32),
                        pltpu.VMEM((rows, 1), F32), pltpu.VMEM((rows, HEAD_DIM), F32)],
        compiler_params=_cparams("parallel", "arbitrary"),
        name=name,
    )(q, cache_k, cache_v, new_k, new_v, logf_t, cum_new, gate)


def _mla_decode_call(q_nope, q_pe, cache_ckv, cache_kpe_t, new_ckv, new_kpe2, layer, w_kvb, g_kn, gate,
                     nb, sd, past, name):
    ts = nb * sd
    rows = past + NEW_PAD
    hq = HEADS * sd
    per_head = 2 * HEAD_DIM
    hps = 4

    def body(qn_ref, qp_ref, cc_ref, ck_ref, nc_ref, nk_ref, w_ref, g_ref, gate_ref, o_ref,
             lat_ref, spe_ref, p_ref, linv_ref, lat_t_ref):
        hp = pl.program_id(1)

        @pl.when(hp == 0)
        def _():
            lat_ref[0:past, :] = cc_ref[...].astype(BF16)
            lat_ref[past:rows, :] = _pad_rows(nc_ref[...], NEW_PAD)
            q_all = jnp.concatenate([qp_ref[:, h * LANES:(h + 1) * LANES] for h in range(HEADS)], axis=0)
            kp = ck_ref[...].astype(BF16)
            spe_ref[:, 0:past] = jnp.dot(q_all, jnp.concatenate([kp, kp], axis=0), preferred_element_type=F32)
            spe_ref[:, past:rows] = lax.dot_general(q_all, _pad_rows(nk_ref[...], NEW_PAD),
                                                    (((1,), (1,)), ((), ())), preferred_element_type=F32)

            lat_t_ref[...] = lat_ref[...].T

        def up_project(pair):
            col = pl.multiple_of(pair * 2 * per_head, 2 * per_head)
            wk = jnp.concatenate([w_ref[:, pl.ds(col, HEAD_DIM)], w_ref[:, pl.ds(col + per_head, HEAD_DIM)]],
                                 axis=1).astype(BF16)
            return lax.dot_general(wk, lat_t_ref[...], (((0,), (0,)), ((), ())), preferred_element_type=F32)

        q_pos = past + lax.broadcasted_iota(jnp.int32, (sd, rows), 0)
        k_pos = lax.broadcasted_iota(jnp.int32, (sd, rows), 1)
        allowed = (k_pos // CHUNK <= q_pos // CHUNK) & (k_pos < past + sd)
        kks = [up_project(hp * (hps // 2) + pr) for pr in range(hps // 2)]
        for g in range(hps):
            hs = slice(g * HEAD_DIM, (g + 1) * HEAD_DIM)
            kk = kks[g // 2][(g % 2) * HEAD_DIM:(g % 2 + 1) * HEAD_DIM, :]
            kn = (_rms(kk, axis=0) * g_ref[...]).astype(BF16)
            r0 = pl.multiple_of((hps * hp + g) * sd, sd)
            s = jnp.dot(qn_ref[:, hs], kn, preferred_element_type=F32)
            s = jnp.where(allowed, s + spe_ref[pl.ds(r0, sd), :], NEG_INF)
            p = jnp.exp2(s - jnp.max(s, axis=-1, keepdims=True))
            linv_ref[pl.ds(r0, sd), :] = 1.0 / jnp.sum(p, axis=-1, keepdims=True)
            p_ref[pl.ds(r0, sd), :] = p.astype(BF16)

        @pl.when(hp == HEADS // hps - 1)
        def _():
            ctx = jnp.dot(p_ref[...], lat_ref[...], preferred_element_type=F32)
            for h in range(HEADS):
                rs = slice(h * sd, (h + 1) * sd)
                wv = w_ref[:, h * per_head + HEAD_DIM:(h + 1) * per_head].astype(BF16)
                out = jnp.dot(ctx[rs, :].astype(BF16), wv, preferred_element_type=F32) * linv_ref[rs, :]
                hs = slice(h * HEAD_DIM, (h + 1) * HEAD_DIM)
                o_ref[:, hs] = (out * gate_ref[:, hs].astype(F32)).astype(BF16)

    return pl.pallas_call(
        body,
        out_shape=jax.ShapeDtypeStruct((ts, WIDTH), BF16),
        grid=(nb, HEADS // hps),
        in_specs=[
            pl.BlockSpec((sd, hps * HEAD_DIM), lambda b, h: (b, h)),
            pl.BlockSpec((sd, WIDTH), lambda b, h: (b, 0)),
            pl.BlockSpec((None, past, KV_LORA), lambda b, h: (layer * nb + b, 0, 0)),
            pl.BlockSpec((None, ROPE_DIM, past), lambda b, h: (layer * nb + b, 0, 0)),
            pl.BlockSpec((sd, KV_LORA), lambda b, h: (b, 0)),
            pl.BlockSpec((sd, LANES), lambda b, h: (b, 0)),
            pl.BlockSpec((None, KV_LORA, HEADS * per_head), lambda b, h: (layer, 0, 0)),
            pl.BlockSpec((HEAD_DIM, 1), lambda b, h: (0, 0)),
            pl.BlockSpec((sd, WIDTH), lambda b, h: (b, 1)),
        ],
        out_specs=pl.BlockSpec((sd, WIDTH), lambda b, h: (b, 0)),
        scratch_shapes=[pltpu.VMEM((rows, KV_LORA), BF16), pltpu.VMEM((hq, rows), F32),
                        pltpu.VMEM((hq, rows), BF16), pltpu.VMEM((hq, 1), F32),
                        pltpu.VMEM((KV_LORA, rows), BF16)],
        compiler_params=_cparams("parallel", "arbitrary"),
        name=name,
    )(q_nope, q_pe, cache_ckv, cache_kpe_t, new_ckv, new_kpe2, w_kvb, g_kn, gate)


def _out_proj_call(o_a, o_b, w_out, layer, x, name):
    t, d = x.shape
    tm, tn = _row_tile(t, 1024), 512

    def body(a_ref, b_ref, wa_ref, wb_ref, x_ref, y_ref):
        acc = jnp.dot(a_ref[...], wa_ref[...].astype(BF16), preferred_element_type=F32)
        acc = acc + jnp.dot(b_ref[...], wb_ref[...].astype(BF16), preferred_element_type=F32)
        y_ref[...] = x_ref[...] + acc

    return pl.pallas_call(
        body,
        out_shape=jax.ShapeDtypeStruct((t, d), F32),
        grid=(t // tm, d // tn),
        in_specs=[
            pl.BlockSpec((tm, WIDTH), lambda i, j: (i, 0)),
            pl.BlockSpec((tm, WIDTH), lambda i, j: (i, 0)),
            pl.BlockSpec((None, WIDTH, tn), lambda i, j: (layer, 0, j)),
            pl.BlockSpec((None, WIDTH, tn), lambda i, j: (layer, 1, j)),
            pl.BlockSpec((tm, tn), lambda i, j: (i, j)),
        ],
        out_specs=pl.BlockSpec((tm, tn), lambda i, j: (i, j)),
        compiler_params=_cparams("parallel", "parallel"),
        name=name,
    )(o_a, o_b, w_out, w_out, x)


def _rope_tables(pos):
    half = ROPE_DIM // 2
    inv_freq = 1.0 / (ROPE_THETA ** (jnp.arange(half, dtype=F32) / half))
    ang = pos.astype(F32)[:, None] * inv_freq[None, :]
    cos, sin = jnp.cos(ang), jnp.sin(ang)
    cos64 = jnp.concatenate([cos, cos], axis=-1)
    sin64 = jnp.concatenate([-sin, sin], axis=-1)
    return jnp.concatenate([cos64, cos64], axis=-1), jnp.concatenate([sin64, sin64], axis=-1)


def _prep_w_in_call(w_in_t, layer, name):
    _, n_in, d = w_in_t.shape
    tc = _row_tile(d, 256)
    src = {}
    o = 0
    for seg, size in (("q", WIDTH), ("k", WIDTH), ("v", WIDTH), ("f", HEADS), ("za", WIDTH),
                      ("cq", Q_LORA), ("ckv", KV_LORA), ("kpe", ROPE_DIM), ("zb", WIDTH)):
        src[seg] = (o, size)
        o += size
    assert o == n_in
    dst = (("q", COL_Q), ("k", COL_K), ("v", COL_V), ("za", COL_GATE), ("zb", COL_GATE + WIDTH), ("cq", COL_CQ),
           ("ckv", COL_CKV), ("kpe", COL_SMALL), ("f", COL_SMALL + LANES))

    def body(x_ref, o_ref):
        for seg, d0 in dst:
            s0, size = src[seg]
            o_ref[d0:d0 + size, :] = x_ref[s0:s0 + size, :].astype(BF16)
        pad0 = COL_SMALL + ROPE_DIM
        o_ref[pad0:COL_SMALL + LANES, :] = jnp.zeros((LANES - ROPE_DIM, tc), BF16)
        pad1 = COL_SMALL + LANES + HEADS
        o_ref[pad1:COL_SMALL + N_SMALL, :] = jnp.zeros((LANES - HEADS, tc), BF16)

    return pl.pallas_call(
        body,
        out_shape=jax.ShapeDtypeStruct((COL_SMALL + N_SMALL, d), BF16),
        grid=(d // tc,),
        in_specs=[pl.BlockSpec((None, n_in, tc), lambda i: (layer, 0, i))],
        out_specs=pl.BlockSpec((COL_SMALL + N_SMALL, tc), lambda i: (0, i)),
        compiler_params=_cparams("parallel"),
        name=name,
    )(w_in_t)


def _prep_w_qb_call(w_qb, layer, name):
    _, k, n = w_qb.shape
    per_head = NOPE_DIM + ROPE_DIM
    tr = _row_tile(k, 256)

    def body(x_ref, o_ref):
        for hh in range(HEADS):
            o_ref[:, hh * NOPE_DIM:(hh + 1) * NOPE_DIM] = x_ref[:, hh * per_head:hh * per_head + NOPE_DIM].astype(BF16)
        for p in range(HEADS // 2):
            pe = [x_ref[:, hh * per_head + NOPE_DIM:(hh + 1) * per_head] for hh in (2 * p, 2 * p + 1)]
            o_ref[:, WIDTH + p * LANES:WIDTH + (p + 1) * LANES] = jnp.concatenate(pe, axis=1).astype(BF16)

    return pl.pallas_call(
        body,
        out_shape=jax.ShapeDtypeStruct((k, n), BF16),
        grid=(k // tr,),
        in_specs=[pl.BlockSpec((None, tr, n), lambda i: (layer, i, 0))],
        out_specs=pl.BlockSpec((tr, n), lambda i: (i, 0)),
        compiler_params=_cparams("parallel"),
        name=name,
    )(w_qb)


def _layer_params(l, g_norm, b_f, g_q_fox, g_k_fox, g_cq, g_qn, g_qp, g_ckv, g_kp, g_kn):
    row = lambda a: a.reshape(1, -1).astype(F32)
    pad = lambda a: jnp.concatenate([row(a), jnp.zeros((1, LANES - a.shape[-1]), F32)], axis=1)
    return dict(
        g_norm=row(g_norm[l]), b_f_row=pad(b_f[l]), g_q=row(g_q_fox[l]), g_k=row(g_k_fox[l]), g_cq=row(g_cq[l]),
        g_qn=row(g_qn[l]), g_qp2=jnp.concatenate([row(g_qp[l]), row(g_qp[l])], axis=1),
        g_ckv=row(g_ckv[l]), g_kp=pad(g_kp[l]), g_kn=row(g_kn[l]),
    )


def _input_stage(x, w, w_qb, p, cos, sin, layer, prev, depth, nb, sl, prompt, tag):
    h = _rmsnorm_call(x, p["g_norm"], f"norm_{tag}")
    prev = prev or {}
    r = {}
    r["q"] = _proj_heads(h, w, COL_Q, p["g_q"], FOX_SCALE, nb, sl, prompt, f"proj_q_{tag}",
                         w_rows_are_outputs=True)
    k_out = _proj_kv_fox(h, w, COL_K, p["g_k"], layer, prev.get("k"), depth, nb, sl, prompt, f"proj_k_{tag}")
    v_out = _proj_kv_fox(h, w, COL_V, None, layer, prev.get("v"), depth, nb, sl, prompt, f"proj_v_{tag}")
    r["k"], r["v"] = k_out[0], v_out[0]
    if prompt:
        r["k_bf"], r["v_bf"] = k_out[1], v_out[1]
    r["gate"] = _proj_gate(h, w, f"proj_gate_{tag}")
    cq = _proj_cq(h, w, p["g_cq"], f"proj_cq_{tag}")
    r["ckv"], r["ckv_bf"] = _proj_ckv(h, w, p["g_ckv"], layer, prev.get("ckv"), depth, f"proj_ckv_{tag}")
    r["kpe"], r["kpe2"], r["logf"], r["cum"] = _proj_small(
        h, w, p["g_kp"], p["b_f_row"], cos, sin, layer, prev.get("kpe"), prev.get("logf"), depth, nb, sl, prompt,
        f"proj_small_{tag}")
    r["q_nope"] = _proj_heads(cq, w_qb, 0, p["g_qn"], MLA_SCALE, nb, sl, prompt, f"proj_qn_{tag}")
    r["q_pe"] = _proj_q_pe(cq, w_qb, p["g_qp2"], cos, sin, nb, sl, prompt, f"proj_qp_{tag}")
    return r


def kernel(x_prompt, x_sample, cache_fox_k, cache_fox_v, cache_fox_logf, cache_mla_ckv, cache_mla_kpe, g_norm,
           w_in, b_f, g_q_fox, g_k_fox, g_cq, w_qb, g_qn, g_qp, g_ckv, g_kp, w_kvb, g_kn, w_out):
    nb, sl, d = x_prompt.shape
    nbd, sd, _ = x_sample.shape
    depth = w_in.shape[0]
    past = cache_fox_k.shape[2]
    assert sl % CHUNK == 0 and past % CHUNK == 0 and sd <= CHUNK, "chunk-aligned streaming shapes only"

    cos_p, sin_p = _rope_tables(jnp.arange(sl, dtype=jnp.int32))
    cos_p, sin_p = jnp.tile(cos_p, (nb, 1)), jnp.tile(sin_p, (nb, 1))
    cos_s, sin_s = _rope_tables(past + jnp.arange(sd, dtype=jnp.int32))
    cos_s, sin_s = jnp.tile(cos_s, (nbd, 1)), jnp.tile(sin_s, (nbd, 1))

    ck2 = cache_fox_k.reshape(-1, HEAD_DIM)
    cv2 = cache_fox_v.reshape(-1, HEAD_DIM)
    logf_t = jnp.swapaxes(cache_fox_logf, 2, 3)
    w_in_t = jnp.swapaxes(w_in, 1, 2)
    cckv = cache_mla_ckv.reshape(depth * nbd, past, KV_LORA)
    ckpe = jnp.swapaxes(cache_mla_kpe, 2, 3).reshape(depth * nbd, ROPE_DIM, past)

    y_p = x_prompt.reshape(nb * sl, d)
    y_s = x_sample.reshape(nbd * sd, d)
    prev_p, prev_s = None, None
    for l in range(depth):
        p = _layer_params(l, g_norm, b_f, g_q_fox, g_k_fox, g_cq, g_qn, g_qp, g_ckv, g_kp, g_kn)
        w = _prep_w_in_call(w_in_t, l, f"prep_w_in_{l}")
        wq = _prep_w_qb_call(w_qb, l, f"prep_w_qb_{l}")
        rp = _input_stage(y_p, w, wq, p, cos_p, sin_p, l, prev_p, depth, nb, sl, True, f"p{l}")
        kn_p, v_p = _proj_kv_up(rp["ckv_bf"], w_kvb, l, p["g_kn"], nb, sl, f"kv_up_p{l}")
        o_a = _flash_call([rp["q"]], rp["k_bf"], rp["cum"].reshape(nb, sl, LANES), rp["v_bf"], rp["gate"], 0,
                          nb, sl, True, f"fox_attn_p{l}")
        o_b = _flash_call([rp["q_nope"], rp["q_pe"]], kn_p, rp["kpe2"].reshape(nb, sl, LANES), v_p, rp["gate"],
                          HEADS, nb, sl, False, f"mla_attn_p{l}")
        y_p = _out_proj_call(o_a, o_b, w_out, l, y_p, f"out_proj_p{l}")
        prev_p = rp
        rs = _input_stage(y_s, w, wq, p, cos_s, sin_s, l, prev_s, depth, nbd, sd, False, f"s{l}")
        o_a = _fox_decode_call(rs["q"], ck2, cv2, rs["k"], rs["v"], l, logf_t, rs["cum"], rs["gate"], nbd, sd,
                               past, f"fox_attn_s{l}")
        o_b = _mla_decode_call(rs["q_nope"], rs["q_pe"], cckv, ckpe, rs["ckv_bf"], rs["kpe2"], l, w_kvb,
                               p["g_kn"].reshape(HEAD_DIM, 1), rs["gate"], nbd, sd, past, f"mla_attn_s{l}")
        y_s = _out_proj_call(o_a, o_b, w_out, l, y_s, f"out_proj_s{l}")
        prev_s = rs

    hk = (HEADS, HEAD_DIM)
    return (y_p.reshape(nb, sl, d), y_s.reshape(nbd, sd, d),
            prev_p["k"].reshape((depth, nb, sl) + hk), prev_p["v"].reshape((depth, nb, sl) + hk),
            prev_p["logf"].reshape(depth, nb, sl, HEADS), prev_p["ckv"].reshape(depth, nb, sl, KV_LORA),
            prev_p["kpe"].reshape(depth, nb, sl, ROPE_DIM),
            prev_s["k"].reshape((depth, nbd, sd) + hk), prev_s["v"].reshape((depth, nbd, sd) + hk),
            prev_s["logf"].reshape(depth, nbd, sd, HEADS), prev_s["ckv"].reshape(depth, nbd, sd, KV_LORA),
            prev_s["kpe"].reshape(depth, nbd, sd, ROPE_DIM))
```
